```python
import math
import jax, jax.numpy as jnp
from jax import lax
import numpy as np

D_MODEL = 2048
BATCH = 8
SEQ = 2048
DEPTH = 1

CHUNK = 64
D_MIX = D_MODEL
D_SSM = D_MIX // 2
D_SGU = D_MIX - D_SSM
SSM_GROUP = 16
SSM_GROUPS = D_SSM // SSM_GROUP
SSM_STATE = 64
SGU_BLOCK = 128
SGU_HEADS = 8
SGU_HEAD_DIM = D_SGU // SGU_HEADS
D_FF = 4 * D_MODEL
D_IN = 2 * D_SSM + 2 * D_SGU
EPS = 1e-5
DT_MIN = 1e-3
DT_MAX = 1e-1

kernel_name = 'hybrid_s5_gmlp_parallel_heads'


def rms_norm(x, gain):
    xf = x.astype(jnp.float32)
    y = xf * lax.rsqrt(jnp.mean(xf * xf, axis=-1, keepdims=True) + EPS)
    return (y * gain.astype(jnp.float32)).astype(x.dtype)


def _complex_linear_combine(left, right):
    ar1, ai1, br1, bi1 = left
    ar2, ai2, br2, bi2 = right
    ar = ar2 * ar1 - ai2 * ai1
    ai = ar2 * ai1 + ai2 * ar1
    br = ar2 * br1 - ai2 * bi1 + br2
    bi = ar2 * bi1 + ai2 * br1 + bi2
    return (ar, ai, br, bi)


def s5_heads(u, g, a_re, a_im, log_dt, b_re, b_im, c_re, c_im, d_skip):
    bsz, seq, _ = u.shape
    f32 = jnp.float32
    uf = u.astype(f32).reshape(bsz, seq, SSM_GROUPS, SSM_GROUP)
    dt = jnp.exp(log_dt.astype(f32))[:, None]
    lam_re = jnp.minimum(a_re.astype(f32), -1e-4)
    lam_im = a_im.astype(f32)
    decay = jnp.exp(lam_re * dt)
    abar_re = decay * jnp.cos(lam_im * dt)
    abar_im = decay * jnp.sin(lam_im * dt)
    den = lam_re * lam_re + lam_im * lam_im
    num_re = abar_re - 1.0
    coef_re = (num_re * lam_re + abar_im * lam_im) / den
    coef_im = (abar_im * lam_re - num_re * lam_im) / den
    br = b_re.astype(f32)
    bi = b_im.astype(f32)
    bbar_re = coef_re[..., None] * br - coef_im[..., None] * bi
    bbar_im = coef_re[..., None] * bi + coef_im[..., None] * br
    bu_re = jnp.einsum('gnh,blgh->blgn', bbar_re, uf)
    bu_im = jnp.einsum('gnh,blgh->blgn', bbar_im, uf)
    a_seq_re = jnp.broadcast_to(abar_re[None, None], (1, seq) + abar_re.shape)
    a_seq_im = jnp.broadcast_to(abar_im[None, None], (1, seq) + abar_im.shape)
    _, _, s_re, s_im = lax.associative_scan(
        _complex_linear_combine, (a_seq_re, a_seq_im, bu_re, bu_im), axis=1)
    y = (jnp.einsum('ghn,blgn->blgh', c_re.astype(f32), s_re)
         - jnp.einsum('ghn,blgn->blgh', c_im.astype(f32), s_im)
         + d_skip.astype(f32).reshape(SSM_GROUPS, SSM_GROUP) * uf)
    y = y.reshape(bsz, seq, D_SSM)
    return (jax.nn.gelu(y) * jax.nn.sigmoid(g.astype(f32))).astype(u.dtype)


def sgu_heads(z, v_gain, w_s, b_s):
    bsz, seq, _ = z.shape
    z = jax.nn.gelu(z)
    u, v = jnp.split(z, 2, axis=-1)
    v = rms_norm(v, v_gain)
    v = v.reshape(bsz, seq // SGU_BLOCK, SGU_BLOCK, SGU_HEADS, SGU_HEAD_DIM)
    chunk_id = jnp.arange(SGU_BLOCK) // CHUNK
    mask = (chunk_id[None, :] <= chunk_id[:, None]).astype(w_s.dtype)
    w = w_s * mask[None]
    mixed = jnp.einsum('hij,bnjhc->bnihc', w, v) + b_s.T[None, None, :, :, None]
    return u * mixed.reshape(bsz, seq, D_SGU)


def setup_inputs(seed: int = 0) -> dict:
    key = jax.random.key(seed)
    ks = jax.random.split(key, 24)
    f32 = jnp.float32
    n_idx = jnp.arange(SSM_STATE, dtype=f32)
    x = jax.random.normal(ks[0], (BATCH, SEQ, D_MODEL), f32)
    w_in = jax.random.normal(ks[1], (DEPTH, D_MODEL, D_IN), f32) * D_MODEL ** -0.5
    ssm_a_re = -0.5 * (1.0 + 0.05 * jax.random.normal(ks[2], (DEPTH, SSM_GROUPS, SSM_STATE), f32))
    ssm_a_im = math.pi * n_idx[None, None, :] + 0.01 * jax.random.normal(ks[3], (DEPTH, SSM_GROUPS, SSM_STATE), f32)
    ssm_log_dt = jax.random.uniform(ks[4], (DEPTH, SSM_GROUPS), f32, math.log(DT_MIN), math.log(DT_MAX))
    ssm_b_re = jax.random.normal(ks[5], (DEPTH, SSM_GROUPS, SSM_STATE, SSM_GROUP), f32) * (2 * SSM_GROUP) ** -0.5
    ssm_b_im = jax.random.normal(ks[6], (DEPTH, SSM_GROUPS, SSM_STATE, SSM_GROUP), f32) * (2 * SSM_GROUP) ** -0.5
    ssm_c_re = jax.random.normal(ks[7], (DEPTH, SSM_GROUPS, SSM_GROUP, SSM_STATE), f32) * SSM_STATE ** -0.5
    ssm_c_im = jax.random.normal(ks[8], (DEPTH, SSM_GROUPS, SSM_GROUP, SSM_STATE), f32) * SSM_STATE ** -0.5
    ssm_d = jax.random.normal(ks[9], (DEPTH, D_SSM), f32)
    sgu_v_gain = 1.0 + 0.02 * jax.random.normal(ks[10], (DEPTH, D_SGU), f32)
    sgu_w = jax.random.normal(ks[11], (DEPTH, SGU_HEADS, SGU_BLOCK, SGU_BLOCK), f32) * SGU_BLOCK ** -0.5
    sgu_b = 1.0 + 0.1 * jax.random.normal(ks[12], (DEPTH, SGU_HEADS, SGU_BLOCK), f32)
    out_gain_ssm = 1.0 + 0.02 * jax.random.normal(ks[13], (DEPTH, D_SSM), f32)
    out_gain_sgu = 1.0 + 0.02 * jax.random.normal(ks[14], (DEPTH, D_SGU), f32)
    w_out = jax.random.normal(ks[15], (DEPTH, D_MIX, D_MODEL), f32) * D_MIX ** -0.5
    mix_norm_gain = 1.0 + 0.02 * jax.random.normal(ks[16], (DEPTH, D_MODEL), f32)
    ffn_norm_gain = 1.0 + 0.02 * jax.random.normal(ks[17], (DEPTH, D_MODEL), f32)
    w_ff1 = jax.random.normal(ks[18], (DEPTH, D_MODEL, D_FF), f32) * D_MODEL ** -0.5
    w_ff2 = jax.random.normal(ks[19], (DEPTH, D_FF, D_MODEL), f32) * D_FF ** -0.5
    final_norm_gain = 1.0 + 0.02 * jax.random.normal(ks[20], (D_MODEL,), f32)
    return {'x': x, 'w_in': w_in, 'ssm_a_re': ssm_a_re, 'ssm_a_im': ssm_a_im,
            'ssm_log_dt': ssm_log_dt, 'ssm_b_re': ssm_b_re, 'ssm_b_im': ssm_b_im,
            'ssm_c_re': ssm_c_re, 'ssm_c_im': ssm_c_im, 'ssm_d': ssm_d,
            'sgu_v_gain': sgu_v_gain, 'sgu_w': sgu_w, 'sgu_b': sgu_b,
            'out_gain_ssm': out_gain_ssm, 'out_gain_sgu': out_gain_sgu, 'w_out': w_out,
            'mix_norm_gain': mix_norm_gain, 'ffn_norm_gain': ffn_norm_gain,
            'w_ff1': w_ff1, 'w_ff2': w_ff2, 'final_norm_gain': final_norm_gain}


def reference(x, w_in, ssm_a_re, ssm_a_im, ssm_log_dt, ssm_b_re, ssm_b_im, ssm_c_re, ssm_c_im,
              ssm_d, sgu_v_gain, sgu_w, sgu_b, out_gain_ssm, out_gain_sgu, w_out,
              mix_norm_gain, ffn_norm_gain, w_ff1, w_ff2, final_norm_gain):
    h = x
    for layer in range(DEPTH):
        hn = rms_norm(h, mix_norm_gain[layer])
        proj = hn @ w_in[layer]
        u_a, g_a, z_b = jnp.split(proj, [D_SSM, 2 * D_SSM], axis=-1)
        y_a = s5_heads(u_a, g_a, ssm_a_re[layer], ssm_a_im[layer], ssm_log_dt[layer],
                       ssm_b_re[layer], ssm_b_im[layer], ssm_c_re[layer], ssm_c_im[layer],
                       ssm_d[layer])
        y_b = sgu_heads(z_b, sgu_v_gain[layer], sgu_w[layer], sgu_b[layer])
        mixed = jnp.concatenate([rms_norm(y_a, out_gain_ssm[layer]),
                                 rms_norm(y_b, out_gain_sgu[layer])], axis=-1)
        h = h + mixed @ w_out[layer]
        hn = rms_norm(h, ffn_norm_gain[layer])
        h = h + jnp.square(jax.nn.relu(hn @ w_ff1[layer])) @ w_ff2[layer]
    return rms_norm(h, final_norm_gain)
```

```python
import functools
import math

import jax
import jax.numpy as jnp
from jax import lax
from jax.experimental import pallas as pl
from jax.experimental.pallas import tpu as pltpu

D_MODEL = 2048
BATCH = 8
SEQ = 2048
CHUNK = 64
D_SSM = 1024
D_SGU = 1024
SSM_GROUP = 16
SSM_GROUPS = 64
SSM_STATE = 64
SGU_BLOCK = 128
SGU_HEADS = 8
SGU_HEAD_DIM = 128
D_FF = 4 * D_MODEL
D_IN = 2 * D_SSM + 2 * D_SGU
EPS = 1e-5

TIME_TILE = SGU_BLOCK
ROWS = BATCH * TIME_TILE
N_TILES = SEQ // TIME_TILE
GROUPS_PER_CHUNK = 16
N_CHUNKS = SSM_GROUPS // GROUPS_PER_CHUNK
CHUNK_IN = GROUPS_PER_CHUNK * SSM_GROUP
CHUNK_ST = GROUPS_PER_CHUNK * SSM_STATE
FF_TILE = 512
PROJ_TILE = 1024
VMEM_LIMIT = 56 * 1024 * 1024

F32 = jnp.float32
BF16 = jnp.bfloat16


def _rms(x, gain):
    return x * lax.rsqrt(jnp.mean(x * x, axis=-1, keepdims=True) + EPS) * gain


def _gelu(x):
    c = math.sqrt(2.0 / math.pi)
    return 0.5 * x * (1.0 + jnp.tanh(c * (x + 0.044715 * (x * x * x))))


def _sigmoid(x):
    return 1.0 / (1.0 + jnp.exp(-x))


def _inproj_kernel(x_ref, gain_ref, w_ref, o_ref, hn_ref):
    @pl.when(pl.program_id(1) == 0)
    def _():
        x = x_ref[...].reshape(ROWS, D_MODEL)
        hn_ref[...] = _rms(x, gain_ref[...]).astype(BF16)

    o_ref[...] = jnp.dot(hn_ref[...], w_ref[...], preferred_element_type=F32).astype(BF16)


def _inproj(x, gain, w_in):
    return pl.pallas_call(
        _inproj_kernel,
        grid=(N_TILES, D_IN // PROJ_TILE),
        in_specs=[
            pl.BlockSpec((BATCH, TIME_TILE, D_MODEL), lambda i, j: (0, i, 0)),
            pl.BlockSpec((1, D_MODEL), lambda i, j: (0, 0)),
            pl.BlockSpec((D_MODEL, PROJ_TILE), lambda i, j: (0, j)),
        ],
        out_specs=pl.BlockSpec((None, ROWS, PROJ_TILE), lambda i, j: (i, 0, j)),
        out_shape=jax.ShapeDtypeStruct((N_TILES, ROWS, D_IN), BF16),
        scratch_shapes=[pltpu.VMEM((ROWS, D_MODEL), BF16)],
        compiler_params=pltpu.CompilerParams(
            dimension_semantics=("arbitrary", "arbitrary"),
            vmem_limit_bytes=VMEM_LIMIT),
        name="inproj",
    )(x, gain, w_in)


def _mixer_kernel(ua_ref, ga_ref, ub_ref, vb_ref, perm_ref, perm_t_ref, bd_ref, cd_ref,
                  are_ref, aim_ref, dskip_ref, vgain_ref, w_ref, bs_ref, gain_a_ref,
                  gain_b_ref, o_ref, state_ref, bu_ref, y_ref):
    @pl.when(pl.program_id(0) == 0)
    def _():
        state_ref[...] = jnp.zeros_like(state_ref)

    u_tb = jnp.dot(perm_ref[...], ua_ref[...], preferred_element_type=F32).astype(BF16)
    for kb in range(N_CHUNKS):
        bu_ref[...] = jnp.dot(u_tb[:, kb * CHUNK_IN:(kb + 1) * CHUNK_IN], bd_ref[kb],
                              preferred_element_type=F32)
        a_re = are_ref[kb]
        a_im = aim_ref[kb]

        def step(t, carry):
            s_re, s_im = carry
            r = pl.multiple_of(t * BATCH, BATCH)
            b_re = bu_ref[pl.ds(r, BATCH), 0:CHUNK_ST]
            b_im = bu_ref[pl.ds(r, BATCH), CHUNK_ST:2 * CHUNK_ST]
            n_re = a_re * s_re - a_im * s_im + b_re
            n_im = a_re * s_im + a_im * s_re + b_im
            bu_ref[pl.ds(r, BATCH), 0:CHUNK_ST] = n_re
            bu_ref[pl.ds(r, BATCH), CHUNK_ST:2 * CHUNK_ST] = n_im
            return n_re, n_im

        s0 = state_ref[kb]
        s_re, s_im = lax.fori_loop(0, TIME_TILE, step,
                                   (s0[:, 0:CHUNK_ST], s0[:, CHUNK_ST:2 * CHUNK_ST]))
        state_ref[kb, :, 0:CHUNK_ST] = s_re
        state_ref[kb, :, CHUNK_ST:2 * CHUNK_ST] = s_im
        y_ref[:, kb * CHUNK_IN:(kb + 1) * CHUNK_IN] = jnp.dot(
            bu_ref[...].astype(BF16), cd_ref[kb], preferred_element_type=F32)

    y_ref[...] = jnp.dot(perm_t_ref[...], y_ref[...].astype(BF16), preferred_element_type=F32)

    for b in range(BATCH):
        rows = pl.ds(b * TIME_TILE, TIME_TILE)
        y = y_ref[rows, :] + dskip_ref[...] * ua_ref[rows, :].astype(F32)
        ya = _gelu(y) * _sigmoid(ga_ref[rows, :].astype(F32))
        o_ref[rows, 0:D_SSM] = _rms(ya, gain_a_ref[...]).astype(BF16)

    row_chunk = lax.broadcasted_iota(jnp.int32, (SGU_BLOCK, SGU_BLOCK), 0) // CHUNK
    col_chunk = lax.broadcasted_iota(jnp.int32, (SGU_BLOCK, SGU_BLOCK), 1) // CHUNK
    causal = col_chunk <= row_chunk
    w_heads = [jnp.where(causal, w_ref[h], 0.0).astype(BF16) for h in range(SGU_HEADS)]
    for b in range(BATCH):
        rows = pl.ds(b * TIME_TILE, TIME_TILE)
        zu = _gelu(ub_ref[rows, :].astype(F32))
        zv = _gelu(vb_ref[rows, :].astype(F32))
        v = _rms(zv, vgain_ref[...]).astype(BF16)
        pieces = []
        for h in range(SGU_HEADS):
            cols = slice(h * SGU_HEAD_DIM, (h + 1) * SGU_HEAD_DIM)
            mixed = jnp.dot(w_heads[h], v[:, cols], preferred_element_type=F32) + bs_ref[h]
            pieces.append(zu[:, cols] * mixed)
        yb = jnp.concatenate(pieces, axis=-1)
        o_ref[rows, D_SSM:D_SSM + D_SGU] = _rms(yb, gain_b_ref[...]).astype(BF16)


def _mixer(proj, perm, perm_t, bd, cd, a_re, a_im, d_skip, v_gain, w_s, bs, gain_a, gain_b):
    def const(shape):
        nd = len(shape)
        return pl.BlockSpec(shape, lambda i: (0,) * nd, pipeline_mode=pl.Buffered(1))

    def col(j):
        return pl.BlockSpec((None, ROWS, D_SSM), lambda i: (i, 0, j))

    return pl.pallas_call(
        _mixer_kernel,
        grid=(N_TILES,),
        in_specs=[
            col(0), col(1), col(2), col(3),
            const((ROWS, ROWS)), const((ROWS, ROWS)),
            const((N_CHUNKS, CHUNK_IN, 2 * CHUNK_ST)),
            const((N_CHUNKS, 2 * CHUNK_ST, CHUNK_IN)),
            const((N_CHUNKS, BATCH, CHUNK_ST)), const((N_CHUNKS, BATCH, CHUNK_ST)),
            const((1, D_SSM)), const((1, D_SGU)),
            const((SGU_HEADS, SGU_BLOCK, SGU_BLOCK)),
            const((SGU_HEADS, SGU_BLOCK, SGU_HEAD_DIM)),
            const((1, D_SSM)), const((1, D_SGU)),
        ],
        out_specs=pl.BlockSpec((None, ROWS, D_MODEL), lambda i: (i, 0, 0)),
        out_shape=jax.ShapeDtypeStruct((N_TILES, ROWS, D_MODEL), BF16),
        scratch_shapes=[
            pltpu.VMEM((N_CHUNKS, BATCH, 2 * CHUNK_ST), F32),
            pltpu.VMEM((ROWS, 2 * CHUNK_ST), F32),
            pltpu.VMEM((ROWS, D_SSM), F32),
        ],
        compiler_params=pltpu.CompilerParams(
            dimension_semantics=("arbitrary",),
            vmem_limit_bytes=VMEM_LIMIT),
        name="mixer",
    )(proj, proj, proj, proj, perm, perm_t, bd, cd, a_re, a_im, d_skip, v_gain, w_s, bs,
      gain_a, gain_b)


def _outproj_kernel(m_ref, w_ref, x_ref, o_ref):
    acc = jnp.dot(m_ref[...], w_ref[...], preferred_element_type=F32)
    o_ref[...] = x_ref[...].reshape(ROWS, PROJ_TILE) + acc


def _outproj(mixed, w_out, x):
    return pl.pallas_call(
        _outproj_kernel,
        grid=(N_TILES, D_MODEL // PROJ_TILE),
        in_specs=[
            pl.BlockSpec((None, ROWS, D_MODEL), lambda i, j: (i, 0, 0)),
            pl.BlockSpec((D_MODEL, PROJ_TILE), lambda i, j: (0, j)),
            pl.BlockSpec((BATCH, TIME_TILE, PROJ_TILE), lambda i, j: (0, i, j)),
        ],
        out_specs=pl.BlockSpec((None, ROWS, PROJ_TILE), lambda i, j: (i, 0, j)),
        out_shape=jax.ShapeDtypeStruct((N_TILES, ROWS, D_MODEL), F32),
        compiler_params=pltpu.CompilerParams(
            dimension_semantics=("arbitrary", "arbitrary"),
            vmem_limit_bytes=VMEM_LIMIT),
        name="outproj",
    )(mixed, w_out, x)


def _ffn_kernel(h_ref, gain_ref, w1_ref, w2_ref, fgain_ref, o_ref, hn_ref):
    k = pl.program_id(1)

    @pl.when(k == 0)
    def _():
        h = h_ref[...]
        hn_ref[...] = _rms(h, gain_ref[...]).astype(BF16)
        o_ref[...] = h.reshape(BATCH, TIME_TILE, D_MODEL)

    a = jnp.dot(hn_ref[...], w1_ref[...], preferred_element_type=F32)
    a = jnp.maximum(a, 0.0)
    a = (a * a).astype(BF16)
    for c in range(D_MODEL // FF_TILE):
        cols = slice(c * FF_TILE, (c + 1) * FF_TILE)
        upd = jnp.dot(a, w2_ref[:, cols], preferred_element_type=F32)
        o_ref[:, :, cols] += upd.reshape(BATCH, TIME_TILE, FF_TILE)

    @pl.when(k == pl.num_programs(1) - 1)
    def _():
        acc = o_ref[...].reshape(ROWS, D_MODEL)
        o_ref[...] = _rms(acc, fgain_ref[...]).reshape(BATCH, TIME_TILE, D_MODEL)


def _ffn(h, gain, w1, w2, fgain):
    return pl.pallas_call(
        _ffn_kernel,
        grid=(N_TILES, D_FF // FF_TILE),
        in_specs=[
            pl.BlockSpec((None, ROWS, D_MODEL), lambda i, k: (i, 0, 0)),
            pl.BlockSpec((1, D_MODEL), lambda i, k: (0, 0)),
            pl.BlockSpec((D_MODEL, FF_TILE), lambda i, k: (0, k)),
            pl.BlockSpec((FF_TILE, D_MODEL), lambda i, k: (k, 0)),
            pl.BlockSpec((1, D_MODEL), lambda i, k: (0, 0)),
        ],
        out_specs=pl.BlockSpec((BATCH, TIME_TILE, D_MODEL), lambda i, k: (0, i, 0)),
        out_shape=jax.ShapeDtypeStruct((BATCH, SEQ, D_MODEL), F32),
        scratch_shapes=[pltpu.VMEM((ROWS, D_MODEL), BF16)],
        compiler_params=pltpu.CompilerParams(
            dimension_semantics=("arbitrary", "arbitrary"),
            vmem_limit_bytes=VMEM_LIMIT),
        name="ffn",
    )(h, gain, w1, w2, fgain)


def _ssm_params(a_re, a_im, log_dt, b_re, b_im, c_re, c_im):
    dt = jnp.exp(log_dt.astype(F32))[:, None]
    lam_re = jnp.minimum(a_re.astype(F32), -1e-4)
    lam_im = a_im.astype(F32)
    decay = jnp.exp(lam_re * dt)
    abar_re = decay * jnp.cos(lam_im * dt)
    abar_im = decay * jnp.sin(lam_im * dt)
    den = lam_re * lam_re + lam_im * lam_im
    num_re = abar_re - 1.0
    coef_re = (num_re * lam_re + abar_im * lam_im) / den
    coef_im = (abar_im * lam_re - num_re * lam_im) / den
    br = b_re.astype(F32)
    bi = b_im.astype(F32)
    bbar_re = coef_re[..., None] * br - coef_im[..., None] * bi
    bbar_im = coef_re[..., None] * bi + coef_im[..., None] * br

    eye = jnp.eye(GROUPS_PER_CHUNK, dtype=F32)

    def in_blocks(m):
        m = m.reshape(N_CHUNKS, GROUPS_PER_CHUNK, SSM_STATE, SSM_GROUP)
        m = jnp.einsum('kgnh,gG->kghGn', m, eye)
        return m.reshape(N_CHUNKS, CHUNK_IN, CHUNK_ST)

    def out_blocks(m):
        m = m.reshape(N_CHUNKS, GROUPS_PER_CHUNK, SSM_GROUP, SSM_STATE)
        m = jnp.einsum('kghn,gG->kgnGh', m, eye)
        return m.reshape(N_CHUNKS, CHUNK_ST, CHUNK_IN)

    bd = jnp.concatenate([in_blocks(bbar_re), in_blocks(bbar_im)], axis=-1).astype(BF16)
    cd = jnp.concatenate([out_blocks(c_re.astype(F32)), out_blocks(-c_im.astype(F32))],
                         axis=1).astype(BF16)

    def per_chunk(a):
        a = a.reshape(N_CHUNKS, 1, CHUNK_ST)
        return jnp.broadcast_to(a, (N_CHUNKS, BATCH, CHUNK_ST))

    return bd, cd, per_chunk(abar_re), per_chunk(abar_im)


def _row_permutation():
    r = jnp.arange(ROWS)
    src = (r % BATCH) * TIME_TILE + r // BATCH
    perm = jax.nn.one_hot(src, ROWS, dtype=BF16)
    return perm, perm.T


def kernel(x, w_in, ssm_a_re, ssm_a_im, ssm_log_dt, ssm_b_re, ssm_b_im, ssm_c_re, ssm_c_im,
           ssm_d, sgu_v_gain, sgu_w, sgu_b, out_gain_ssm, out_gain_sgu, w_out,
           mix_norm_gain, ffn_norm_gain, w_ff1, w_ff2, final_norm_gain):
    perm, perm_t = _row_permutation()
    row = lambda v: v.astype(F32).reshape(1, -1)
    assert x.shape == (BATCH, SEQ, D_MODEL) and w_in.shape[0] == 1, "single-layer block only"
    layer = 0
    bd, cd, a_re, a_im = _ssm_params(
        ssm_a_re[layer], ssm_a_im[layer], ssm_log_dt[layer], ssm_b_re[layer],
        ssm_b_im[layer], ssm_c_re[layer], ssm_c_im[layer])
    bs = jnp.broadcast_to(sgu_b[layer].astype(F32)[:, :, None],
                          (SGU_HEADS, SGU_BLOCK, SGU_HEAD_DIM))
    proj = _inproj(x, row(mix_norm_gain[layer]), w_in[layer].astype(BF16))
    mixed = _mixer(proj, perm, perm_t, bd, cd, a_re, a_im, row(ssm_d[layer]),
                   row(sgu_v_gain[layer]), sgu_w[layer].astype(F32), bs,
                   row(out_gain_ssm[layer]), row(out_gain_sgu[layer]))
    h_tiles = _outproj(mixed, w_out[layer].astype(BF16), x)
    return _ffn(h_tiles, row(ffn_norm_gain[layer]), w_ff1[layer].astype(BF16),
                w_ff2[layer].astype(BF16), row(final_norm_gain))
```

```python
import functools
import math

import jax
import jax.numpy as jnp
from jax import lax
from jax.experimental import pallas as pl
from jax.experimental.pallas import tpu as pltpu

D_MODEL = 2048
BATCH = 8
SEQ = 2048
CHUNK = 64
D_SSM = 1024
D_SGU = 1024
SSM_GROUP = 16
SSM_GROUPS = 64
SSM_STATE = 64
SGU_BLOCK = 128
SGU_HEADS = 8
SGU_HEAD_DIM = 128
D_FF = 4 * D_MODEL
D_IN = 2 * D_SSM + 2 * D_SGU
EPS = 1e-5

TIME_TILE = SGU_BLOCK
ROWS = BATCH * TIME_TILE
N_TILES = SEQ // TIME_TILE
GROUPS_PER_CHUNK = 16
N_CHUNKS = SSM_GROUPS // GROUPS_PER_CHUNK
CHUNK_IN = GROUPS_PER_CHUNK * SSM_GROUP
CHUNK_ST = GROUPS_PER_CHUNK * SSM_STATE
FF_TILE = 512
PROJ_TILE = 1024
VMEM_LIMIT = 56 * 1024 * 1024

F32 = jnp.float32
BF16 = jnp.bfloat16


def _rms(x, gain):
    return x * lax.rsqrt(jnp.mean(x * x, axis=-1, keepdims=True) + EPS) * gain


def _gelu(x):
    c = math.sqrt(2.0 / math.pi)
    return 0.5 * x * (1.0 + jnp.tanh(c * (x + 0.044715 * (x * x * x))))


def _sigmoid(x):
    return 1.0 / (1.0 + jnp.exp(-x))


def _inproj_kernel(x_ref, gain_ref, w_ref, o_ref, hn_ref):
    @pl.when(pl.program_id(1) == 0)
    def _():
        x = x_ref[...].reshape(ROWS, D_MODEL)
        hn_ref[...] = _rms(x, gain_ref[...]).astype(BF16)

    o_ref[...] = jnp.dot(hn_ref[...], w_ref[...], preferred_element_type=F32).astype(BF16)


def _inproj(x, gain, w_in):
    return pl.pallas_call(
        _inproj_kernel,
        grid=(N_TILES, D_IN // PROJ_TILE),
        in_specs=[
            pl.BlockSpec((BATCH, TIME_TILE, D_MODEL), lambda i, j: (0, i, 0)),
            pl.BlockSpec((1, D_MODEL), lambda i, j: (0, 0)),
            pl.BlockSpec((D_MODEL, PROJ_TILE), lambda i, j: (0, j)),
        ],
        out_specs=pl.BlockSpec((None, ROWS, PROJ_TILE), lambda i, j: (i, 0, j)),
        out_shape=jax.ShapeDtypeStruct((N_TILES, ROWS, D_IN), BF16),
        scratch_shapes=[pltpu.VMEM((ROWS, D_MODEL), BF16)],
        compiler_params=pltpu.CompilerParams(
            dimension_semantics=("arbitrary", "arbitrary"),
            vmem_limit_bytes=VMEM_LIMIT),
        name="inproj",
    )(x, gain, w_in)


def _sgu_steps(b, ub_ref, vb_ref, vgain_ref, w_heads, bs_ref, gain_b_ref, o_ref):
    rows = pl.ds(b * TIME_TILE, TIME_TILE)
    v = _rms(_gelu(vb_ref[rows, :].astype(F32)), vgain_ref[...]).astype(BF16)
    yield
    pieces = []
    for h in range(SGU_HEADS):
        cols = slice(h * SGU_HEAD_DIM, (h + 1) * SGU_HEAD_DIM)
        mixed = jnp.dot(w_heads[h], v[:, cols], preferred_element_type=F32) + bs_ref[h]
        pieces.append(_gelu(ub_ref[rows, cols].astype(F32)) * mixed)
        yield
    yb = jnp.concatenate(pieces, axis=-1)
    o_ref[rows, D_SSM:D_SSM + D_SGU] = _rms(yb, gain_b_ref[...]).astype(BF16)
    yield


def _advance(gens, n):
    while n > 0 and gens:
        try:
            next(gens[0])
            n -= 1
        except StopIteration:
            gens.pop(0)


def _mixer_kernel(ua_ref, ga_ref, ub_ref, vb_ref, bd_ref, cd_ref, are_ref, aim_ref, dskip_ref,
                  vgain_ref, w_ref, bs_ref, gain_a_ref, gain_b_ref, o_ref,
                  state_ref, bu_ref, utb_ref, y_ref):
    @pl.when(pl.program_id(0) == 0)
    def _():
        state_ref[...] = jnp.zeros_like(state_ref)

    row_chunk = lax.broadcasted_iota(jnp.int32, (SGU_BLOCK, SGU_BLOCK), 0) // CHUNK
    col_chunk = lax.broadcasted_iota(jnp.int32, (SGU_BLOCK, SGU_BLOCK), 1) // CHUNK
    causal = col_chunk <= row_chunk
    w_heads = [jnp.where(causal, w_ref[h], 0.0).astype(BF16) for h in range(SGU_HEADS)]

    for b in range(BATCH):
        u_b = ua_ref[pl.ds(b * TIME_TILE, TIME_TILE), :].astype(F32)
        for s in range(D_SSM // 128):
            utb_ref[s, pl.ds(b, TIME_TILE, stride=BATCH), :] = u_b[:, s * 128:(s + 1) * 128]

    half = CHUNK_ST // 128
    for kb in range(N_CHUNKS):
        side = [_sgu_steps(b, ub_ref, vb_ref, vgain_ref, w_heads, bs_ref, gain_b_ref, o_ref)
                for b in range(2 * kb, 2 * kb + 2)]
        u_tb = jnp.concatenate([utb_ref[2 * kb], utb_ref[2 * kb + 1]], axis=1)
        u_tb16 = u_tb.astype(BF16)
        for jp in range(2 * CHUNK_ST // 256):
            cols = slice(jp * 256, (jp + 1) * 256)
            bu_ref[:, cols] = jnp.dot(u_tb16, bd_ref[kb, :, cols], preferred_element_type=F32)
            _advance(side, 2)
        _advance(side, 100)

        a_re = [are_ref[kb, :, j * 128:(j + 1) * 128] for j in range(half)]
        a_im = [aim_ref[kb, :, j * 128:(j + 1) * 128] for j in range(half)]

        def step(t, carry):
            s_re, s_im = carry
            at_t = pl.ds(pl.multiple_of(t * BATCH, BATCH), BATCH)
            b_re = [bu_ref[at_t, j * 128:(j + 1) * 128] for j in range(half)]
            b_im = [bu_ref[at_t, CHUNK_ST + j * 128:CHUNK_ST + (j + 1) * 128]
                    for j in range(half)]
            n_re = tuple(a_re[j] * s_re[j] - a_im[j] * s_im[j] + b_re[j] for j in range(half))
            n_im = tuple(a_re[j] * s_im[j] + a_im[j] * s_re[j] + b_im[j] for j in range(half))
            for j in range(half):
                bu_ref[at_t, j * 128:(j + 1) * 128] = n_re[j]
                bu_ref[at_t, CHUNK_ST + j * 128:CHUNK_ST + (j + 1) * 128] = n_im[j]
            return n_re, n_im

        s0_re = tuple(state_ref[kb, :, j * 128:(j + 1) * 128] for j in range(half))
        s0_im = tuple(state_ref[kb, :, CHUNK_ST + j * 128:CHUNK_ST + (j + 1) * 128]
                      for j in range(half))
        s_re, s_im = lax.fori_loop(0, TIME_TILE, step, (s0_re, s0_im), unroll=4)
        for j in range(half):
            state_ref[kb, :, j * 128:(j + 1) * 128] = s_re[j]
            state_ref[kb, :, CHUNK_ST + j * 128:CHUNK_ST + (j + 1) * 128] = s_im[j]

        cols = slice(kb * CHUNK_IN, (kb + 1) * CHUNK_IN)
        y_tb = jnp.dot(bu_ref[...].astype(BF16), cd_ref[kb], preferred_element_type=F32)
        y_tb = y_tb + dskip_ref[:, cols] * u_tb
        utb_ref[2 * kb] = y_tb[:, 0:128]
        utb_ref[2 * kb + 1] = y_tb[:, 128:256]
        for b in range(BATCH):
            rows = pl.ds(b * TIME_TILE, TIME_TILE)
            y = jnp.concatenate(
                [utb_ref[2 * kb + s, pl.ds(b, TIME_TILE, stride=BATCH), :] for s in range(2)],
                axis=1)
            y_ref[rows, cols] = _gelu(y) * _sigmoid(ga_ref[rows, cols].astype(F32))

    for b in range(BATCH):
        rows = pl.ds(b * TIME_TILE, TIME_TILE)
        o_ref[rows, 0:D_SSM] = _rms(y_ref[rows, :], gain_a_ref[...]).astype(BF16)


def _mixer(proj, bd, cd, a_re, a_im, d_skip, v_gain, w_s, bs, gain_a, gain_b):
    def const(shape):
        nd = len(shape)
        return pl.BlockSpec(shape, lambda i: (0,) * nd, pipeline_mode=pl.Buffered(1))

    def col(j):
        return pl.BlockSpec((None, ROWS, D_SSM), lambda i: (i, 0, j))

    return pl.pallas_call(
        _mixer_kernel,
        grid=(N_TILES,),
        in_specs=[
            col(0), col(1), col(2), col(3),
            const((N_CHUNKS, CHUNK_IN, 2 * CHUNK_ST)),
            const((N_CHUNKS, 2 * CHUNK_ST, CHUNK_IN)),
            const((N_CHUNKS, BATCH, CHUNK_ST)), const((N_CHUNKS, BATCH, CHUNK_ST)),
            const((1, D_SSM)), const((1, D_SGU)),
            const((SGU_HEADS, SGU_BLOCK, SGU_BLOCK)),
            const((SGU_HEADS, SGU_BLOCK, SGU_HEAD_DIM)),
            const((1, D_SSM)), const((1, D_SGU)),
        ],
        out_specs=pl.BlockSpec((None, ROWS, D_MODEL), lambda i: (i, 0, 0)),
        out_shape=jax.ShapeDtypeStruct((N_TILES, ROWS, D_MODEL), BF16),
        scratch_shapes=[
            pltpu.VMEM((N_CHUNKS, BATCH, 2 * CHUNK_ST), F32),
            pltpu.VMEM((ROWS, 2 * CHUNK_ST), F32),
            pltpu.VMEM((D_SSM // 128, ROWS, 128), F32),
            pltpu.VMEM((ROWS, D_SSM), F32),
        ],
        compiler_params=pltpu.CompilerParams(
            dimension_semantics=("arbitrary",),
            vmem_limit_bytes=VMEM_LIMIT),
        name="mixer",
    )(proj, proj, proj, proj, bd, cd, a_re, a_im, d_skip, v_gain, w_s, bs, gain_a, gain_b)


def _outproj_kernel(m_ref, w_ref, x_ref, o_ref):
    acc = jnp.dot(m_ref[...], w_ref[...], preferred_element_type=F32)
    o_ref[...] = x_ref[...].reshape(ROWS, PROJ_TILE) + acc


def _outproj(mixed, w_out, x):
    return pl.pallas_call(
        _outproj_kernel,
        grid=(N_TILES, D_MODEL // PROJ_TILE),
        in_specs=[
            pl.BlockSpec((None, ROWS, D_MODEL), lambda i, j: (i, 0, 0)),
            pl.BlockSpec((D_MODEL, PROJ_TILE), lambda i, j: (0, j)),
            pl.BlockSpec((BATCH, TIME_TILE, PROJ_TILE), lambda i, j: (0, i, j)),
        ],
        out_specs=pl.BlockSpec((None, ROWS, PROJ_TILE), lambda i, j: (i, 0, j)),
        out_shape=jax.ShapeDtypeStruct((N_TILES, ROWS, D_MODEL), F32),
        compiler_params=pltpu.CompilerParams(
            dimension_semantics=("arbitrary", "arbitrary"),
            vmem_limit_bytes=VMEM_LIMIT),
        name="outproj",
    )(mixed, w_out, x)


def _ffn_kernel(h_ref, gain_ref, w1_ref, w2_ref, fgain_ref, o_ref, hn_ref):
    k = pl.program_id(1)

    @pl.when(k == 0)
    def _():
        h = h_ref[...]
        hn_ref[...] = _rms(h, gain_ref[...]).astype(BF16)
        o_ref[...] = h.reshape(BATCH, TIME_TILE, D_MODEL)

    a = jnp.dot(hn_ref[...], w1_ref[...], preferred_element_type=F32)
    a = jnp.maximum(a, 0.0)
    a = (a * a).astype(BF16)
    for c in range(D_MODEL // FF_TILE):
        cols = slice(c * FF_TILE, (c + 1) * FF_TILE)
        upd = jnp.dot(a, w2_ref[:, cols], preferred_element_type=F32)
        o_ref[:, :, cols] += upd.reshape(BATCH, TIME_TILE, FF_TILE)

    @pl.when(k == pl.num_programs(1) - 1)
    def _():
        acc = o_ref[...].reshape(ROWS, D_MODEL)
        o_ref[...] = _rms(acc, fgain_ref[...]).reshape(BATCH, TIME_TILE, D_MODEL)


def _ffn(h, gain, w1, w2, fgain):
    return pl.pallas_call(
        _ffn_kernel,
        grid=(N_TILES, D_FF // FF_TILE),
        in_specs=[
            pl.BlockSpec((None, ROWS, D_MODEL), lambda i, k: (i, 0, 0)),
            pl.BlockSpec((1, D_MODEL), lambda i, k: (0, 0)),
            pl.BlockSpec((D_MODEL, FF_TILE), lambda i, k: (0, k)),
            pl.BlockSpec((FF_TILE, D_MODEL), lambda i, k: (k, 0)),
            pl.BlockSpec((1, D_MODEL), lambda i, k: (0, 0)),
        ],
        out_specs=pl.BlockSpec((BATCH, TIME_TILE, D_MODEL), lambda i, k: (0, i, 0)),
        out_shape=jax.ShapeDtypeStruct((BATCH, SEQ, D_MODEL), F32),
        scratch_shapes=[pltpu.VMEM((ROWS, D_MODEL), BF16)],
        compiler_params=pltpu.CompilerParams(
            dimension_semantics=("arbitrary", "arbitrary"),
            vmem_limit_bytes=VMEM_LIMIT),
        name="ffn",
    )(h, gain, w1, w2, fgain)


def _ssm_params(a_re, a_im, log_dt, b_re, b_im, c_re, c_im):
    dt = jnp.exp(log_dt.astype(F32))[:, None]
    lam_re = jnp.minimum(a_re.astype(F32), -1e-4)
    lam_im = a_im.astype(F32)
    decay = jnp.exp(lam_re * dt)
    abar_re = decay * jnp.cos(lam_im * dt)
    abar_im = decay * jnp.sin(lam_im * dt)
    den = lam_re * lam_re + lam_im * lam_im
    num_re = abar_re - 1.0
    coef_re = (num_re * lam_re + abar_im * lam_im) / den
    coef_im = (abar_im * lam_re - num_re * lam_im) / den
    br = b_re.astype(F32)
    bi = b_im.astype(F32)
    bbar_re = coef_re[..., None] * br - coef_im[..., None] * bi
    bbar_im = coef_re[..., None] * bi + coef_im[..., None] * br

    eye = jnp.eye(GROUPS_PER_CHUNK, dtype=F32)

    def in_blocks(m):
        m = m.reshape(N_CHUNKS, GROUPS_PER_CHUNK, SSM_STATE, SSM_GROUP)
        m = jnp.einsum('kgnh,gG->kghGn', m, eye)
        return m.reshape(N_CHUNKS, CHUNK_IN, CHUNK_ST)

    def out_blocks(m):
        m = m.reshape(N_CHUNKS, GROUPS_PER_CHUNK, SSM_GROUP, SSM_STATE)
        m = jnp.einsum('kghn,gG->kgnGh', m, eye)
        return m.reshape(N_CHUNKS, CHUNK_ST, CHUNK_IN)

    bd = jnp.concatenate([in_blocks(bbar_re), in_blocks(bbar_im)], axis=-1).astype(BF16)
    cd = jnp.concatenate([out_blocks(c_re.astype(F32)), out_blocks(-c_im.astype(F32))],
                         axis=1).astype(BF16)

    def per_chunk(a):
        a = a.reshape(N_CHUNKS, 1, CHUNK_ST)
        return jnp.broadcast_to(a, (N_CHUNKS, BATCH, CHUNK_ST))

    return bd, cd, per_chunk(abar_re), per_chunk(abar_im)


def kernel(x, w_in, ssm_a_re, ssm_a_im, ssm_log_dt, ssm_b_re, ssm_b_im, ssm_c_re, ssm_c_im,
           ssm_d, sgu_v_gain, sgu_w, sgu_b, out_gain_ssm, out_gain_sgu, w_out,
           mix_norm_gain, ffn_norm_gain, w_ff1, w_ff2, final_norm_gain):
    row = lambda v: v.astype(F32).reshape(1, -1)
    assert x.shape == (BATCH, SEQ, D_MODEL) and w_in.shape[0] == 1, "single-layer block only"
    layer = 0
    bd, cd, a_re, a_im = _ssm_params(
        ssm_a_re[layer], ssm_a_im[layer], ssm_log_dt[layer], ssm_b_re[layer],
        ssm_b_im[layer], ssm_c_re[layer], ssm_c_im[layer])
    bs = jnp.broadcast_to(sgu_b[layer].astype(F32)[:, :, None],
                          (SGU_HEADS, SGU_BLOCK, SGU_HEAD_DIM))
    proj = _inproj(x, row(mix_norm_gain[layer]), w_in[layer].astype(BF16))
    mixed = _mixer(proj, bd, cd, a_re, a_im, row(ssm_d[layer]),
                   row(sgu_v_gain[layer]), sgu_w[layer].astype(F32), bs,
                   row(out_gain_ssm[layer]), row(out_gain_sgu[layer]))
    h_tiles = _outproj(mixed, w_out[layer].astype(BF16), x)
    return _ffn(h_tiles, row(ffn_norm_gain[layer]), w_ff1[layer].astype(BF16),
                w_ff2[layer].astype(BF16), row(final_norm_gain))
```

```python
import functools
import math

import jax
import jax.numpy as jnp
from jax import lax
from jax.experimental import pallas as pl
from jax.experimental.pallas import tpu as pltpu

D_MODEL = 2048
BATCH = 8
SEQ = 2048
CHUNK = 64
D_SSM = 1024
D_SGU = 1024
SSM_GROUP = 16
SSM_GROUPS = 64
SSM_STATE = 64
SGU_BLOCK = 128
SGU_HEADS = 8
SGU_HEAD_DIM = 128
D_FF = 4 * D_MODEL
D_IN = 2 * D_SSM + 2 * D_SGU
EPS = 1e-5

TIME_TILE = SGU_BLOCK
ROWS = BATCH * TIME_TILE
N_TILES = SEQ // TIME_TILE
GROUPS_PER_CHUNK = 16
N_CHUNKS = SSM_GROUPS // GROUPS_PER_CHUNK
CHUNK_IN = GROUPS_PER_CHUNK * SSM_GROUP
CHUNK_ST = GROUPS_PER_CHUNK * SSM_STATE
GATING_PLAN = ((0,), (1, 2), (3, 4), (5, 6), (7,))
FF_TILE = 512
PROJ_TILE = 1024
N_PROJ_TILES = D_IN // PROJ_TILE
X_PART = D_MODEL // N_PROJ_TILES
N_CAST_STEPS = N_TILES * N_PROJ_TILES
VMEM_LIMIT = 56 * 1024 * 1024

F32 = jnp.float32
BF16 = jnp.bfloat16


def _rms(x, gain):
    return x * lax.rsqrt(jnp.mean(x * x, axis=-1, keepdims=True) + EPS) * gain


def _gelu(x):
    c = math.sqrt(2.0 / math.pi)
    return 0.5 * x * (1.0 + jnp.tanh(c * (x + 0.044715 * (x * x * x))))


def _sigmoid(x):
    return 1.0 / (1.0 + jnp.exp(-x))


def _inproj_kernel(x0_ref, x1_ref, x2_ref, x3_ref, gain_ref, w_ref, wff1_ref, wff2_ref, wout_ref,
                   o_ref, wff1_o, wff2_o, wout_o, hn_ref):
    j = pl.program_id(1)

    @pl.when(j == 0)
    def _():
        parts = [r[...].reshape(ROWS, X_PART) for r in (x0_ref, x1_ref, x2_ref, x3_ref)]
        ssq = sum(jnp.sum(p * p, axis=-1, keepdims=True) for p in parts)
        scale = lax.rsqrt(ssq * (1.0 / D_MODEL) + EPS)
        for q, p in enumerate(parts):
            cols = slice(q * X_PART, (q + 1) * X_PART)
            hn_ref[:, cols] = (p * scale * gain_ref[:, cols]).astype(BF16)

    o_ref[...] = jnp.dot(hn_ref[...], w_ref[j], preferred_element_type=F32).astype(BF16)
    wff1_o[...] = wff1_ref[...].astype(BF16)
    wff2_o[...] = wff2_ref[...].astype(BF16)
    wout_o[...] = wout_ref[...].astype(BF16)


def _inproj(x, gain, w_in_tiles, w_ff1, w_ff2, w_out):
    def x_part(q):
        return pl.BlockSpec(
            (BATCH, TIME_TILE, X_PART),
            lambda i, j: (0, jnp.minimum(i + jnp.where(j > q, 1, 0), N_TILES - 1), q))

    def cast_rows(rows, width):
        return pl.BlockSpec((rows // N_CAST_STEPS, width),
                            lambda i, j: (i * N_PROJ_TILES + j, 0))

    return pl.pallas_call(
        _inproj_kernel,
        grid=(N_TILES, N_PROJ_TILES),
        in_specs=[
            x_part(0), x_part(1), x_part(2), x_part(3),
            pl.BlockSpec((1, D_MODEL), lambda i, j: (0, 0)),
            pl.BlockSpec((N_PROJ_TILES, D_MODEL, PROJ_TILE), lambda i, j: (0, 0, 0),
                         pipeline_mode=pl.Buffered(1)),
            cast_rows(D_MODEL, D_FF), cast_rows(D_FF, D_MODEL), cast_rows(D_MODEL, D_MODEL),
        ],
        out_specs=[
            pl.BlockSpec((None, ROWS, PROJ_TILE), lambda i, j: (i, 0, j)),
            cast_rows(D_MODEL, D_FF), cast_rows(D_FF, D_MODEL), cast_rows(D_MODEL, D_MODEL),
        ],
        out_shape=[
            jax.ShapeDtypeStruct((N_TILES, ROWS, D_IN), BF16),
            jax.ShapeDtypeStruct((D_MODEL, D_FF), BF16),
            jax.ShapeDtypeStruct((D_FF, D_MODEL), BF16),
            jax.ShapeDtypeStruct((D_MODEL, D_MODEL), BF16),
        ],
        scratch_shapes=[pltpu.VMEM((ROWS, D_MODEL), BF16)],
        compiler_params=pltpu.CompilerParams(
            dimension_semantics=("arbitrary", "arbitrary"),
            vmem_limit_bytes=VMEM_LIMIT),
        name="inproj",
    )(x, x, x, x, gain, w_in_tiles, w_ff1, w_ff2, w_out)


def _sgu_steps(b, ub_ref, vb_ref, vgain_ref, w_heads, bs_ref, gain_b_ref, o_ref):
    rows = pl.ds(b * TIME_TILE, TIME_TILE)
    v = _rms(_gelu(vb_ref[rows, :].astype(F32)), vgain_ref[...]).astype(BF16)
    yield
    pieces = []
    for h in range(SGU_HEADS):
        cols = slice(h * SGU_HEAD_DIM, (h + 1) * SGU_HEAD_DIM)
        mixed = jnp.dot(w_heads[h], v[:, cols], preferred_element_type=F32) + bs_ref[h]
        pieces.append(_gelu(ub_ref[rows, cols].astype(F32)) * mixed)
        yield
    yb = jnp.concatenate(pieces, axis=-1)
    o_ref[rows, D_SSM:D_SSM + D_SGU] = _rms(yb, gain_b_ref[...]).astype(BF16)
    yield


def _advance(gens, n):
    while n > 0 and gens:
        try:
            next(gens[0])
            n -= 1
        except StopIteration:
            gens.pop(0)


def _mixer_kernel(ua_ref, ga_ref, ub_ref, vb_ref, bd_ref, cd_ref, are_ref, aim_ref, dskip_ref,
                  vgain_ref, w_ref, bs_ref, gain_a_ref, gain_b_ref, o_ref,
                  state_ref, bu_ref, utb_ref, y_ref):
    @pl.when(pl.program_id(0) == 0)
    def _():
        state_ref[...] = jnp.zeros_like(state_ref)

    row_chunk = lax.broadcasted_iota(jnp.int32, (SGU_BLOCK, SGU_BLOCK), 0) // CHUNK
    col_chunk = lax.broadcasted_iota(jnp.int32, (SGU_BLOCK, SGU_BLOCK), 1) // CHUNK
    causal = col_chunk <= row_chunk
    w_heads = [jnp.where(causal, w_ref[h], 0.0).astype(BF16) for h in range(SGU_HEADS)]

    for b in range(BATCH):
        u_b = ua_ref[pl.ds(b * TIME_TILE, TIME_TILE), :].astype(F32)
        for s in range(D_SSM // 128):
            utb_ref[s, pl.ds(b, TIME_TILE, stride=BATCH), :] = u_b[:, s * 128:(s + 1) * 128]

    half = CHUNK_ST // 128

    def gating(batches):
        return [_sgu_steps(b, ub_ref, vb_ref, vgain_ref, w_heads, bs_ref, gain_b_ref, o_ref)
                for b in batches]

    def input_matmul(kb, side):
        u_tb = jnp.concatenate([utb_ref[2 * kb], utb_ref[2 * kb + 1]], axis=1)
        u_tb16 = u_tb.astype(BF16)
        for jp in range(2 * CHUNK_ST // 256):
            cols = slice(jp * 256, (jp + 1) * 256)
            bu_ref[:, cols] = jnp.dot(u_tb16, bd_ref[kb, :, cols], preferred_element_type=F32)
            _advance(side, 2)
        return u_tb

    side = gating(GATING_PLAN[0])
    u_tb = input_matmul(0, side)
    _advance(side, 100)
    for kb in range(N_CHUNKS):
        a_re = [are_ref[kb, :, j * 128:(j + 1) * 128] for j in range(half)]
        a_im = [aim_ref[kb, :, j * 128:(j + 1) * 128] for j in range(half)]

        def step(t, carry):
            s_re, s_im = carry
            at_t = pl.ds(pl.multiple_of(t * BATCH, BATCH), BATCH)
            b_re = [bu_ref[at_t, j * 128:(j + 1) * 128] for j in range(half)]
            b_im = [bu_ref[at_t, CHUNK_ST + j * 128:CHUNK_ST + (j + 1) * 128]
                    for j in range(half)]
            n_re = tuple(a_re[j] * s_re[j] - a_im[j] * s_im[j] + b_re[j] for j in range(half))
            n_im = tuple(a_re[j] * s_im[j] + a_im[j] * s_re[j] + b_im[j] for j in range(half))
            for j in range(half):
                bu_ref[at_t, j * 128:(j + 1) * 128] = n_re[j]
                bu_ref[at_t, CHUNK_ST + j * 128:CHUNK_ST + (j + 1) * 128] = n_im[j]
            return n_re, n_im

        s0_re = tuple(state_ref[kb, :, j * 128:(j + 1) * 128] for j in range(half))
        s0_im = tuple(state_ref[kb, :, CHUNK_ST + j * 128:CHUNK_ST + (j + 1) * 128]
                      for j in range(half))
        s_re, s_im = lax.fori_loop(0, TIME_TILE, step, (s0_re, s0_im), unroll=4)
        for j in range(half):
            state_ref[kb, :, j * 128:(j + 1) * 128] = s_re[j]
            state_ref[kb, :, CHUNK_ST + j * 128:CHUNK_ST + (j + 1) * 128] = s_im[j]

        side = gating(GATING_PLAN[kb + 1])
        cols = slice(kb * CHUNK_IN, (kb + 1) * CHUNK_IN)
        y_re = jnp.dot(bu_ref[:, 0:CHUNK_ST].astype(BF16), cd_ref[kb, 0:CHUNK_ST, :],
                       preferred_element_type=F32)
        y_im = jnp.dot(bu_ref[:, CHUNK_ST:2 * CHUNK_ST].astype(BF16),
                       cd_ref[kb, CHUNK_ST:2 * CHUNK_ST, :], preferred_element_type=F32)
        _advance(side, 5)
        y_tb = y_re + y_im + dskip_ref[:, cols] * u_tb
        utb_ref[2 * kb] = y_tb[:, 0:128]
        utb_ref[2 * kb + 1] = y_tb[:, 128:256]
        if kb + 1 < N_CHUNKS:
            u_tb = input_matmul(kb + 1, side)
        for b in range(BATCH):
            rows = pl.ds(b * TIME_TILE, TIME_TILE)
            y = jnp.concatenate(
                [utb_ref[2 * kb + s, pl.ds(b, TIME_TILE, stride=BATCH), :] for s in range(2)],
                axis=1)
            y_ref[rows, cols] = _gelu(y) * _sigmoid(ga_ref[rows, cols].astype(F32))
        _advance(side, 100)

    for b in range(BATCH):
        rows = pl.ds(b * TIME_TILE, TIME_TILE)
        o_ref[rows, 0:D_SSM] = _rms(y_ref[rows, :], gain_a_ref[...]).astype(BF16)


def _mixer(proj, bd, cd, a_re, a_im, d_skip, v_gain, w_s, bs, gain_a, gain_b):
    def const(shape):
        nd = len(shape)
        return pl.BlockSpec(shape, lambda i: (0,) * nd, pipeline_mode=pl.Buffered(1))

    def col(j):
        return pl.BlockSpec((None, ROWS, D_SSM), lambda i: (i, 0, j))

    return pl.pallas_call(
        _mixer_kernel,
        grid=(N_TILES,),
        in_specs=[
            col(0), col(1), col(2), col(3),
            const((N_CHUNKS, CHUNK_IN, 2 * CHUNK_ST)),
            const((N_CHUNKS, 2 * CHUNK_ST, CHUNK_IN)),
            const((N_CHUNKS, BATCH, CHUNK_ST)), const((N_CHUNKS, BATCH, CHUNK_ST)),
            const((1, D_SSM)), const((1, D_SGU)),
            const((SGU_HEADS, SGU_BLOCK, SGU_BLOCK)),
            const((SGU_HEADS, SGU_BLOCK, SGU_HEAD_DIM)),
            const((1, D_SSM)), const((1, D_SGU)),
        ],
        out_specs=pl.BlockSpec((None, ROWS, D_MODEL), lambda i: (i, 0, 0)),
        out_shape=jax.ShapeDtypeStruct((N_TILES, ROWS, D_MODEL), BF16),
        scratch_shapes=[
            pltpu.VMEM((N_CHUNKS, BATCH, 2 * CHUNK_ST), F32),
            pltpu.VMEM((ROWS, 2 * CHUNK_ST), F32),
            pltpu.VMEM((D_SSM // 128, ROWS, 128), F32),
            pltpu.VMEM((ROWS, D_SSM), F32),
        ],
        compiler_params=pltpu.CompilerParams(
            dimension_semantics=("arbitrary",),
            vmem_limit_bytes=VMEM_LIMIT),
        name="mixer",
    )(proj, proj, proj, proj, bd, cd, a_re, a_im, d_skip, v_gain, w_s, bs, gain_a, gain_b)


def _outproj_kernel(m_ref, w_ref, x_ref, o_ref):
    acc = jnp.dot(m_ref[...], w_ref[...], preferred_element_type=F32)
    o_ref[...] = x_ref[...].reshape(ROWS, PROJ_TILE) + acc


def _outproj(mixed, w_out, x):
    return pl.pallas_call(
        _outproj_kernel,
        grid=(N_TILES, D_MODEL // PROJ_TILE),
        in_specs=[
            pl.BlockSpec((None, ROWS, D_MODEL), lambda i, j: (i, 0, 0)),
            pl.BlockSpec((D_MODEL, PROJ_TILE), lambda i, j: (0, j)),
            pl.BlockSpec((BATCH, TIME_TILE, PROJ_TILE), lambda i, j: (0, i, j)),
        ],
        out_specs=pl.BlockSpec((None, ROWS, PROJ_TILE), lambda i, j: (i, 0, j)),
        out_shape=jax.ShapeDtypeStruct((N_TILES, ROWS, D_MODEL), F32),
        compiler_params=pltpu.CompilerParams(
            dimension_semantics=("arbitrary", "arbitrary"),
            vmem_limit_bytes=VMEM_LIMIT),
        name="outproj",
    )(mixed, w_out, x)


def _ffn_kernel(h_ref, gain_ref, w1_ref, w2_ref, fgain_ref, o_ref, hn_ref):
    k = pl.program_id(1)

    @pl.when(k == 0)
    def _():
        h = h_ref[...]
        hn_ref[...] = _rms(h, gain_ref[...]).astype(BF16)
        o_ref[...] = h.reshape(BATCH, TIME_TILE, D_MODEL)

    a = jnp.dot(hn_ref[...], w1_ref[...], preferred_element_type=F32)
    a = jnp.maximum(a, 0.0)
    a = (a * a).astype(BF16)
    for c in range(D_MODEL // FF_TILE):
        cols = slice(c * FF_TILE, (c + 1) * FF_TILE)
        upd = jnp.dot(a, w2_ref[:, cols], preferred_element_type=F32)
        o_ref[:, :, cols] += upd.reshape(BATCH, TIME_TILE, FF_TILE)

    @pl.when(k == pl.num_programs(1) - 1)
    def _():
        acc = o_ref[...].reshape(ROWS, D_MODEL)
        o_ref[...] = _rms(acc, fgain_ref[...]).reshape(BATCH, TIME_TILE, D_MODEL)


def _ffn(h, gain, w1, w2, fgain):
    return pl.pallas_call(
        _ffn_kernel,
        grid=(N_TILES, D_FF // FF_TILE),
        in_specs=[
            pl.BlockSpec((None, ROWS, D_MODEL), lambda i, k: (i, 0, 0)),
            pl.BlockSpec((1, D_MODEL), lambda i, k: (0, 0)),
            pl.BlockSpec((D_MODEL, FF_TILE), lambda i, k: (0, k)),
            pl.BlockSpec((FF_TILE, D_MODEL), lambda i, k: (k, 0)),
            pl.BlockSpec((1, D_MODEL), lambda i, k: (0, 0)),
        ],
        out_specs=pl.BlockSpec((BATCH, TIME_TILE, D_MODEL), lambda i, k: (0, i, 0)),
        out_shape=jax.ShapeDtypeStruct((BATCH, SEQ, D_MODEL), F32),
        scratch_shapes=[pltpu.VMEM((ROWS, D_MODEL), BF16)],
        compiler_params=pltpu.CompilerParams(
            dimension_semantics=("arbitrary", "arbitrary"),
            vmem_limit_bytes=VMEM_LIMIT),
        name="ffn",
    )(h, gain, w1, w2, fgain)


def _ssm_params(a_re, a_im, log_dt, b_re, b_im, c_re, c_im):
    dt = jnp.exp(log_dt.astype(F32))[:, None]
    lam_re = jnp.minimum(a_re.astype(F32), -1e-4)
    lam_im = a_im.astype(F32)
    decay = jnp.exp(lam_re * dt)
    abar_re = decay * jnp.cos(lam_im * dt)
    abar_im = decay * jnp.sin(lam_im * dt)
    den = lam_re * lam_re + lam_im * lam_im
    num_re = abar_re - 1.0
    coef_re = (num_re * lam_re + abar_im * lam_im) / den
    coef_im = (abar_im * lam_re - num_re * lam_im) / den
    br = b_re.astype(F32)
    bi = b_im.astype(F32)
    bbar_re = coef_re[..., None] * br - coef_im[..., None] * bi
    bbar_im = coef_re[..., None] * bi + coef_im[..., None] * br

    def block_diag(m):
        r, c = m.shape[2:]
        rows = m.reshape(N_CHUNKS, GROUPS_PER_CHUNK * r, 1, c)
        tiled = jnp.broadcast_to(rows, (N_CHUNKS, GROUPS_PER_CHUNK * r, GROUPS_PER_CHUNK, c))
        row_group = lax.broadcasted_iota(jnp.int32, tiled.shape, 1) // r
        col_group = lax.broadcasted_iota(jnp.int32, tiled.shape, 2)
        tiled = jnp.where(row_group == col_group, tiled, 0.0)
        return tiled.reshape(N_CHUNKS, GROUPS_PER_CHUNK * r, GROUPS_PER_CHUNK * c)

    def in_blocks(m):
        m = m.reshape(N_CHUNKS, GROUPS_PER_CHUNK, SSM_STATE, SSM_GROUP)
        return block_diag(jnp.swapaxes(m, 2, 3))

    def out_blocks(m):
        m = m.reshape(N_CHUNKS, GROUPS_PER_CHUNK, SSM_GROUP, SSM_STATE)
        return block_diag(jnp.swapaxes(m, 2, 3))

    bd = jnp.concatenate([in_blocks(bbar_re), in_blocks(bbar_im)], axis=-1).astype(BF16)
    cd = jnp.concatenate([out_blocks(c_re.astype(F32)), out_blocks(-c_im.astype(F32))],
                         axis=1).astype(BF16)

    def per_chunk(a):
        a = a.reshape(N_CHUNKS, 1, CHUNK_ST)
        return jnp.broadcast_to(a, (N_CHUNKS, BATCH, CHUNK_ST))

    return bd, cd, per_chunk(abar_re), per_chunk(abar_im)


def kernel(x, w_in, ssm_a_re, ssm_a_im, ssm_log_dt, ssm_b_re, ssm_b_im, ssm_c_re, ssm_c_im,
           ssm_d, sgu_v_gain, sgu_w, sgu_b, out_gain_ssm, out_gain_sgu, w_out,
           mix_norm_gain, ffn_norm_gain, w_ff1, w_ff2, final_norm_gain):
    row = lambda v: v.astype(F32).reshape(1, -1)
    assert x.shape == (BATCH, SEQ, D_MODEL) and w_in.shape[0] == 1, "single-layer block only"
    layer = 0
    bd, cd, a_re, a_im = _ssm_params(
        ssm_a_re[layer], ssm_a_im[layer], ssm_log_dt[layer], ssm_b_re[layer],
        ssm_b_im[layer], ssm_c_re[layer], ssm_c_im[layer])
    bs = jnp.broadcast_to(sgu_b[layer].astype(F32)[:, :, None],
                          (SGU_HEADS, SGU_BLOCK, SGU_HEAD_DIM))
    w_in_tiles = jnp.swapaxes(
        w_in[layer].astype(BF16).reshape(D_MODEL, N_PROJ_TILES, PROJ_TILE), 0, 1)
    proj, w_ff1_16, w_ff2_16, w_out_16 = _inproj(
        x, row(mix_norm_gain[layer]), w_in_tiles, w_ff1[layer], w_ff2[layer], w_out[layer])
    mixed = _mixer(proj, bd, cd, a_re, a_im, row(ssm_d[layer]),
                   row(sgu_v_gain[layer]), sgu_w[layer].astype(F32), bs,
                   row(out_gain_ssm[layer]), row(out_gain_sgu[layer]))
    h_tiles = _outproj(mixed, w_out_16, x)
    return _ffn(h_tiles, row(ffn_norm_gain[layer]), w_ff1_16, w_ff2_16, row(final_norm_gain))
```

```python
import functools
import math

import jax
import jax.numpy as jnp
from jax import lax
from jax.experimental import pallas as pl
from jax.experimental.pallas import tpu as pltpu

D_MODEL = 2048
BATCH = 8
SEQ = 2048
CHUNK = 64
D_SSM = 1024
D_SGU = 1024
SSM_GROUP = 16
SSM_GROUPS = 64
SSM_STATE = 64
SGU_BLOCK = 128
SGU_HEADS = 8
SGU_HEAD_DIM = 128
D_FF = 4 * D_MODEL
D_IN = 2 * D_SSM + 2 * D_SGU
EPS = 1e-5

TIME_TILE = SGU_BLOCK
ROWS = BATCH * TIME_TILE
N_TILES = SEQ // TIME_TILE
GROUPS_PER_CHUNK = 16
N_CHUNKS = SSM_GROUPS // GROUPS_PER_CHUNK
CHUNK_IN = GROUPS_PER_CHUNK * SSM_GROUP
CHUNK_ST = GROUPS_PER_CHUNK * SSM_STATE
GATING_PLAN = ((0,), (1, 2), (3, 4), (5, 6), (7,))
SCAN_UNROLL = 8
FF_TILE = 512
PROJ_TILE = 1024
N_PROJ_TILES = D_IN // PROJ_TILE
X_PART = D_MODEL // N_PROJ_TILES
N_CAST_STEPS = N_TILES * N_PROJ_TILES
VMEM_LIMIT = 56 * 1024 * 1024

F32 = jnp.float32
BF16 = jnp.bfloat16


def _rms(x, gain):
    return x * lax.rsqrt(jnp.mean(x * x, axis=-1, keepdims=True) + EPS) * gain


def _gelu(x):
    c = math.sqrt(2.0 / math.pi)
    return 0.5 * x * (1.0 + jnp.tanh(c * (x + 0.044715 * (x * x * x))))


def _sigmoid(x):
    return 1.0 / (1.0 + jnp.exp(-x))


def _inproj_kernel(x0_ref, x1_ref, x2_ref, x3_ref, gain_ref, w_ref, wff1_ref, wff2_ref, wout_ref,
                   o_ref, wff1_o, wff2_o, wout_o, hn_ref):
    j = pl.program_id(1)

    @pl.when(j == 0)
    def _():
        parts = [r[...].reshape(ROWS, X_PART) for r in (x0_ref, x1_ref, x2_ref, x3_ref)]
        ssq = sum(jnp.sum(p * p, axis=-1, keepdims=True) for p in parts)
        scale = lax.rsqrt(ssq * (1.0 / D_MODEL) + EPS)
        for q, p in enumerate(parts):
            cols = slice(q * X_PART, (q + 1) * X_PART)
            hn_ref[:, cols] = (p * scale * gain_ref[:, cols]).astype(BF16)

    o_ref[...] = jnp.dot(hn_ref[...], w_ref[j], preferred_element_type=F32).astype(BF16)
    wff1_o[...] = wff1_ref[...].astype(BF16)
    wff2_o[...] = wff2_ref[...].astype(BF16)
    wout_o[...] = wout_ref[...].astype(BF16)


def _inproj(x, gain, w_in_tiles, w_ff1, w_ff2, w_out):
    def x_part(q):
        return pl.BlockSpec(
            (BATCH, TIME_TILE, X_PART),
            lambda i, j: (0, jnp.minimum(i + jnp.where(j > q, 1, 0), N_TILES - 1), q))

    def cast_rows(rows, width):
        return pl.BlockSpec((rows // N_CAST_STEPS, width),
                            lambda i, j: (i * N_PROJ_TILES + j, 0))

    return pl.pallas_call(
        _inproj_kernel,
        grid=(N_TILES, N_PROJ_TILES),
        in_specs=[
            x_part(0), x_part(1), x_part(2), x_part(3),
            pl.BlockSpec((1, D_MODEL), lambda i, j: (0, 0)),
            pl.BlockSpec((N_PROJ_TILES, D_MODEL, PROJ_TILE), lambda i, j: (0, 0, 0),
                         pipeline_mode=pl.Buffered(1)),
            cast_rows(D_MODEL, D_FF), cast_rows(D_FF, D_MODEL), cast_rows(D_MODEL, D_MODEL),
        ],
        out_specs=[
            pl.BlockSpec((None, ROWS, PROJ_TILE), lambda i, j: (i, 0, j)),
            cast_rows(D_MODEL, D_FF), cast_rows(D_FF, D_MODEL), cast_rows(D_MODEL, D_MODEL),
        ],
        out_shape=[
            jax.ShapeDtypeStruct((N_TILES, ROWS, D_IN), BF16),
            jax.ShapeDtypeStruct((D_MODEL, D_FF), BF16),
            jax.ShapeDtypeStruct((D_FF, D_MODEL), BF16),
            jax.ShapeDtypeStruct((D_MODEL, D_MODEL), BF16),
        ],
        scratch_shapes=[pltpu.VMEM((ROWS, D_MODEL), BF16)],
        compiler_params=pltpu.CompilerParams(
            dimension_semantics=("arbitrary", "arbitrary"),
            vmem_limit_bytes=VMEM_LIMIT),
        name="inproj",
    )(x, x, x, x, gain, w_in_tiles, w_ff1, w_ff2, w_out)


def _sgu_steps(b, ub_ref, vb_ref, vgain_ref, w_heads, bs_ref, gain_b_ref, o_ref):
    rows = pl.ds(b * TIME_TILE, TIME_TILE)
    v = _rms(_gelu(vb_ref[rows, :].astype(F32)), vgain_ref[...]).astype(BF16)
    yield
    pieces = []
    for h in range(SGU_HEADS):
        cols = slice(h * SGU_HEAD_DIM, (h + 1) * SGU_HEAD_DIM)
        mixed = jnp.dot(w_heads[h], v[:, cols], preferred_element_type=F32) + bs_ref[h]
        pieces.append(_gelu(ub_ref[rows, cols].astype(F32)) * mixed)
        yield
    yb = jnp.concatenate(pieces, axis=-1)
    o_ref[rows, D_SSM:D_SSM + D_SGU] = _rms(yb, gain_b_ref[...]).astype(BF16)
    yield


def _advance(gens, n):
    while n > 0 and gens:
        try:
            next(gens[0])
            n -= 1
        except StopIteration:
            gens.pop(0)


def _mixer_kernel(ua_ref, ga_ref, ub_ref, vb_ref, bd_ref, cd_ref, are_ref, aim_ref, dskip_ref,
                  vgain_ref, w_ref, bs_ref, gain_a_ref, gain_b_ref, o_ref,
                  state_ref, bu_ref, utb_ref, y_ref):
    @pl.when(pl.program_id(0) == 0)
    def _():
        state_ref[...] = jnp.zeros_like(state_ref)

    row_chunk = lax.broadcasted_iota(jnp.int32, (SGU_BLOCK, SGU_BLOCK), 0) // CHUNK
    col_chunk = lax.broadcasted_iota(jnp.int32, (SGU_BLOCK, SGU_BLOCK), 1) // CHUNK
    causal = col_chunk <= row_chunk
    w_heads = [jnp.where(causal, w_ref[h], 0.0).astype(BF16) for h in range(SGU_HEADS)]

    for b in range(BATCH):
        u_b = ua_ref[pl.ds(b * TIME_TILE, TIME_TILE), :].astype(F32)
        for s in range(D_SSM // 128):
            utb_ref[s, pl.ds(b, TIME_TILE, stride=BATCH), :] = u_b[:, s * 128:(s + 1) * 128]

    half = CHUNK_ST // 128

    def gating(batches):
        return [_sgu_steps(b, ub_ref, vb_ref, vgain_ref, w_heads, bs_ref, gain_b_ref, o_ref)
                for b in batches]

    def input_matmul(kb, side):
        u_tb = jnp.concatenate([utb_ref[2 * kb], utb_ref[2 * kb + 1]], axis=1)
        u_tb16 = u_tb.astype(BF16)
        for jp in range(2 * CHUNK_ST // 256):
            cols = slice(jp * 256, (jp + 1) * 256)
            bu_ref[:, cols] = jnp.dot(u_tb16, bd_ref[kb, :, cols], preferred_element_type=F32)
            _advance(side, 2)
        return u_tb

    side = gating(GATING_PLAN[0])
    u_tb = input_matmul(0, side)
    _advance(side, 100)
    for kb in range(N_CHUNKS):
        a_re = [are_ref[kb, :, j * 128:(j + 1) * 128] for j in range(half)]
        a_im = [aim_ref[kb, :, j * 128:(j + 1) * 128] for j in range(half)]

        def step(t, carry):
            s_re, s_im = carry
            at_t = pl.ds(pl.multiple_of(t * BATCH, BATCH), BATCH)
            b_re = [bu_ref[at_t, j * 128:(j + 1) * 128] for j in range(half)]
            b_im = [bu_ref[at_t, CHUNK_ST + j * 128:CHUNK_ST + (j + 1) * 128]
                    for j in range(half)]
            n_re = tuple(a_re[j] * s_re[j] - a_im[j] * s_im[j] + b_re[j] for j in range(half))
            n_im = tuple(a_re[j] * s_im[j] + a_im[j] * s_re[j] + b_im[j] for j in range(half))
            for j in range(half):
                bu_ref[at_t, j * 128:(j + 1) * 128] = n_re[j]
                bu_ref[at_t, CHUNK_ST + j * 128:CHUNK_ST + (j + 1) * 128] = n_im[j]
            return n_re, n_im

        s0_re = tuple(state_ref[kb, :, j * 128:(j + 1) * 128] for j in range(half))
        s0_im = tuple(state_ref[kb, :, CHUNK_ST + j * 128:CHUNK_ST + (j + 1) * 128]
                      for j in range(half))
        s_re, s_im = lax.fori_loop(0, TIME_TILE, step, (s0_re, s0_im), unroll=SCAN_UNROLL)
        for j in range(half):
            state_ref[kb, :, j * 128:(j + 1) * 128] = s_re[j]
            state_ref[kb, :, CHUNK_ST + j * 128:CHUNK_ST + (j + 1) * 128] = s_im[j]

        side = gating(GATING_PLAN[kb + 1])
        cols = slice(kb * CHUNK_IN, (kb + 1) * CHUNK_IN)
        y_re = jnp.dot(bu_ref[:, 0:CHUNK_ST].astype(BF16), cd_ref[kb, 0:CHUNK_ST, :],
                       preferred_element_type=F32)
        y_im = jnp.dot(bu_ref[:, CHUNK_ST:2 * CHUNK_ST].astype(BF16),
                       cd_ref[kb, CHUNK_ST:2 * CHUNK_ST, :], preferred_element_type=F32)
        _advance(side, 5)
        y_tb = (y_re + dskip_ref[:, cols] * u_tb) - y_im
        utb_ref[2 * kb] = y_tb[:, 0:128]
        utb_ref[2 * kb + 1] = y_tb[:, 128:256]
        if kb + 1 < N_CHUNKS:
            u_tb = input_matmul(kb + 1, side)
        for b in range(BATCH):
            rows = pl.ds(b * TIME_TILE, TIME_TILE)
            y = jnp.concatenate(
                [utb_ref[2 * kb + s, pl.ds(b, TIME_TILE, stride=BATCH), :] for s in range(2)],
                axis=1)
            y_ref[rows, cols] = _gelu(y) * _sigmoid(ga_ref[rows, cols].astype(F32))
        _advance(side, 100)

    for b in range(BATCH):
        rows = pl.ds(b * TIME_TILE, TIME_TILE)
        o_ref[rows, 0:D_SSM] = _rms(y_ref[rows, :], gain_a_ref[...]).astype(BF16)


def _mixer(proj, bd, cd, a_re, a_im, d_skip, v_gain, w_s, bs, gain_a, gain_b):
    def const(shape):
        nd = len(shape)
        return pl.BlockSpec(shape, lambda i: (0,) * nd, pipeline_mode=pl.Buffered(1))

    def col(j):
        return pl.BlockSpec((None, ROWS, D_SSM), lambda i: (i, 0, j))

    return pl.pallas_call(
        _mixer_kernel,
        grid=(N_TILES,),
        in_specs=[
            col(0), col(1), col(2), col(3),
            const((N_CHUNKS, CHUNK_IN, 2 * CHUNK_ST)),
            const((N_CHUNKS, 2 * CHUNK_ST, CHUNK_IN)),
            const((N_CHUNKS, BATCH, CHUNK_ST)), const((N_CHUNKS, BATCH, CHUNK_ST)),
            const((1, D_SSM)), const((1, D_SGU)),
            const((SGU_HEADS, SGU_BLOCK, SGU_BLOCK)),
            const((SGU_HEADS, SGU_BLOCK, SGU_HEAD_DIM)),
            const((1, D_SSM)), const((1, D_SGU)),
        ],
        out_specs=pl.BlockSpec((None, ROWS, D_MODEL), lambda i: (i, 0, 0)),
        out_shape=jax.ShapeDtypeStruct((N_TILES, ROWS, D_MODEL), BF16),
        scratch_shapes=[
            pltpu.VMEM((N_CHUNKS, BATCH, 2 * CHUNK_ST), F32),
            pltpu.VMEM((ROWS, 2 * CHUNK_ST), F32),
            pltpu.VMEM((D_SSM // 128, ROWS, 128), F32),
            pltpu.VMEM((ROWS, D_SSM), F32),
        ],
        compiler_params=pltpu.CompilerParams(
            dimension_semantics=("arbitrary",),
            vmem_limit_bytes=VMEM_LIMIT),
        name="mixer",
    )(proj, proj, proj, proj, bd, cd, a_re, a_im, d_skip, v_gain, w_s, bs, gain_a, gain_b)


def _outproj_kernel(m_ref, w_ref, x_ref, o_ref):
    for c in range(D_MODEL // PROJ_TILE):
        cols = slice(c * PROJ_TILE, (c + 1) * PROJ_TILE)
        acc = jnp.dot(m_ref[...], w_ref[:, cols], preferred_element_type=F32)
        o_ref[:, cols] = x_ref[:, :, cols].reshape(ROWS, PROJ_TILE) + acc


def _outproj(mixed, w_out, x):
    return pl.pallas_call(
        _outproj_kernel,
        grid=(N_TILES,),
        in_specs=[
            pl.BlockSpec((None, ROWS, D_MODEL), lambda i: (i, 0, 0)),
            pl.BlockSpec((D_MODEL, D_MODEL), lambda i: (0, 0), pipeline_mode=pl.Buffered(1)),
            pl.BlockSpec((BATCH, TIME_TILE, D_MODEL), lambda i: (0, i, 0)),
        ],
        out_specs=pl.BlockSpec((None, ROWS, D_MODEL), lambda i: (i, 0, 0)),
        out_shape=jax.ShapeDtypeStruct((N_TILES, ROWS, D_MODEL), F32),
        compiler_params=pltpu.CompilerParams(
            dimension_semantics=("arbitrary",),
            vmem_limit_bytes=VMEM_LIMIT),
        name="outproj",
    )(mixed, w_out, x)


def _ffn_kernel(h_ref, gain_ref, w1_ref, w2_ref, fgain_ref, o_ref, hn_ref):
    k = pl.program_id(1)

    @pl.when(k == 0)
    def _():
        h = h_ref[...]
        hn_ref[...] = _rms(h, gain_ref[...]).astype(BF16)
        o_ref[...] = h.reshape(BATCH, TIME_TILE, D_MODEL)

    a = jnp.dot(hn_ref[...], w1_ref[...], preferred_element_type=F32)
    a = jnp.maximum(a, 0.0)
    a = (a * a).astype(BF16)
    for c in range(D_MODEL // FF_TILE):
        cols = slice(c * FF_TILE, (c + 1) * FF_TILE)
        upd = jnp.dot(a, w2_ref[:, cols], preferred_element_type=F32)
        o_ref[:, :, cols] += upd.reshape(BATCH, TIME_TILE, FF_TILE)

    @pl.when(k == pl.num_programs(1) - 1)
    def _():
        acc = o_ref[...].reshape(ROWS, D_MODEL)
        o_ref[...] = _rms(acc, fgain_ref[...]).reshape(BATCH, TIME_TILE, D_MODEL)


def _ffn(h, gain, w1, w2, fgain):
    return pl.pallas_call(
        _ffn_kernel,
        grid=(N_TILES, D_FF // FF_TILE),
        in_specs=[
            pl.BlockSpec((None, ROWS, D_MODEL), lambda i, k: (i, 0, 0)),
            pl.BlockSpec((1, D_MODEL), lambda i, k: (0, 0)),
            pl.BlockSpec((D_MODEL, FF_TILE), lambda i, k: (0, k)),
            pl.BlockSpec((FF_TILE, D_MODEL), lambda i, k: (k, 0)),
            pl.BlockSpec((1, D_MODEL), lambda i, k: (0, 0)),
        ],
        out_specs=pl.BlockSpec((BATCH, TIME_TILE, D_MODEL), lambda i, k: (0, i, 0)),
        out_shape=jax.ShapeDtypeStruct((BATCH, SEQ, D_MODEL), F32),
        scratch_shapes=[pltpu.VMEM((ROWS, D_MODEL), BF16)],
        compiler_params=pltpu.CompilerParams(
            dimension_semantics=("arbitrary", "arbitrary"),
            vmem_limit_bytes=VMEM_LIMIT),
        name="ffn",
    )(h, gain, w1, w2, fgain)


def _ssm_params(a_re, a_im, log_dt, b_re, b_im, c_re, c_im):
    dt = jnp.exp(log_dt.astype(F32))[:, None]
    lam_re = jnp.minimum(a_re.astype(F32), -1e-4)
    lam_im = a_im.astype(F32)
    decay = jnp.exp(lam_re * dt)
    abar_re = decay * jnp.cos(lam_im * dt)
    abar_im = decay * jnp.sin(lam_im * dt)
    den = lam_re * lam_re + lam_im * lam_im
    num_re = abar_re - 1.0
    coef_re = (num_re * lam_re + abar_im * lam_im) / den
    coef_im = (abar_im * lam_re - num_re * lam_im) / den
    br = b_re.astype(F32)
    bi = b_im.astype(F32)
    bbar_re = coef_re[..., None] * br - coef_im[..., None] * bi
    bbar_im = coef_re[..., None] * bi + coef_im[..., None] * br

    def block_diag(m):
        r, c = m.shape[2:]
        rows = m.reshape(N_CHUNKS, GROUPS_PER_CHUNK * r, 1, c)
        tiled = jnp.broadcast_to(rows, (N_CHUNKS, GROUPS_PER_CHUNK * r, GROUPS_PER_CHUNK, c))
        row_group = lax.broadcasted_iota(jnp.int32, tiled.shape, 1) // r
        col_group = lax.broadcasted_iota(jnp.int32, tiled.shape, 2)
        tiled = jnp.where(row_group == col_group, tiled, 0.0)
        return tiled.reshape(N_CHUNKS, GROUPS_PER_CHUNK * r, GROUPS_PER_CHUNK * c)

    def in_blocks(m):
        m = m.reshape(N_CHUNKS, GROUPS_PER_CHUNK, SSM_STATE, SSM_GROUP)
        return block_diag(jnp.swapaxes(m, 2, 3))

    def out_blocks(m):
        m = m.reshape(N_CHUNKS, GROUPS_PER_CHUNK, SSM_GROUP, SSM_STATE)
        return block_diag(jnp.swapaxes(m, 2, 3))

    bd = jnp.concatenate([in_blocks(bbar_re), in_blocks(bbar_im)], axis=-1).astype(BF16)
    cd = jnp.concatenate([out_blocks(c_re.astype(F32)), out_blocks(c_im.astype(F32))],
                         axis=1).astype(BF16)

    def per_chunk(a):
        a = a.reshape(N_CHUNKS, 1, CHUNK_ST)
        return jnp.broadcast_to(a, (N_CHUNKS, BATCH, CHUNK_ST))

    return bd, cd, per_chunk(abar_re), per_chunk(abar_im)


def kernel(x, w_in, ssm_a_re, ssm_a_im, ssm_log_dt, ssm_b_re, ssm_b_im, ssm_c_re, ssm_c_im,
           ssm_d, sgu_v_gain, sgu_w, sgu_b, out_gain_ssm, out_gain_sgu, w_out,
           mix_norm_gain, ffn_norm_gain, w_ff1, w_ff2, final_norm_gain):
    row = lambda v: v.astype(F32).reshape(1, -1)
    assert x.shape == (BATCH, SEQ, D_MODEL) and w_in.shape[0] == 1, "single-layer block only"
    layer = 0
    bd, cd, a_re, a_im = _ssm_params(
        ssm_a_re[layer], ssm_a_im[layer], ssm_log_dt[layer], ssm_b_re[layer],
        ssm_b_im[layer], ssm_c_re[layer], ssm_c_im[layer])
    bs = jnp.broadcast_to(sgu_b[layer].astype(F32)[:, :, None],
                          (SGU_HEADS, SGU_BLOCK, SGU_HEAD_DIM))
    w_in_tiles = jnp.swapaxes(
        w_in[layer].astype(BF16).reshape(D_MODEL, N_PROJ_TILES, PROJ_TILE), 0, 1)
    proj, w_ff1_16, w_ff2_16, w_out_16 = _inproj(
        x, row(mix_norm_gain[layer]), w_in_tiles, w_ff1[layer], w_ff2[layer], w_out[layer])
    mixed = _mixer(proj, bd, cd, a_re, a_im, row(ssm_d[layer]),
                   row(sgu_v_gain[layer]), sgu_w[layer].astype(F32), bs,
                   row(out_gain_ssm[layer]), row(out_gain_sgu[layer]))
    h_tiles = _outproj(mixed, w_out_16, x)
    return _ffn(h_tiles, row(ffn_norm_gain[layer]), w_ff1_16, w_ff2_16, row(final_norm_gain))
```

```python
import math

import jax
import jax.numpy as jnp
from jax import lax
from jax.experimental import pallas as pl
from jax.experimental.pallas import tpu as pltpu

D_MODEL = 2048
BATCH = 8
SEQ = 2048
CHUNK = 64
D_SSM = 1024
D_SGU = 1024
SSM_GROUP = 16
SSM_GROUPS = 64
SSM_STATE = 64
SGU_BLOCK = 128
SGU_HEADS = 8
SGU_HEAD_DIM = 128
D_FF = 4 * D_MODEL
D_IN = 2 * D_SSM + 2 * D_SGU
EPS = 1e-5

TIME_TILE = SGU_BLOCK
ROWS = BATCH * TIME_TILE
N_TILES = SEQ // TIME_TILE
GROUPS_PER_CHUNK = 16
N_CHUNKS = SSM_GROUPS // GROUPS_PER_CHUNK
CHUNK_IN = GROUPS_PER_CHUNK * SSM_GROUP
CHUNK_ST = GROUPS_PER_CHUNK * SSM_STATE
GATING_PLAN = ((0,), (1, 2), (3, 4), (5, 6), (7,))
SCAN_UNROLL = 8
FF_TILE = 512
H_PART = D_MODEL // 2
H_PART_SWITCH = (7, D_FF // FF_TILE - 1)
PROJ_TILE = 1024
N_PROJ_TILES = D_IN // PROJ_TILE
X_PART = D_MODEL // N_PROJ_TILES
N_CAST_STEPS = N_TILES * N_PROJ_TILES
VMEM_LIMIT = 56 * 1024 * 1024

F32 = jnp.float32
BF16 = jnp.bfloat16


def _rms(x, gain):
    return x * lax.rsqrt(jnp.mean(x * x, axis=-1, keepdims=True) + EPS) * gain


def _gelu(x):
    c = math.sqrt(2.0 / math.pi)
    return 0.5 * x * (1.0 + jnp.tanh(c * (x + 0.044715 * (x * x * x))))


def _sigmoid(x):
    return 1.0 / (1.0 + jnp.exp(-x))


def _inproj_kernel(x0_ref, x1_ref, x2_ref, x3_ref, gain_ref, w_ref, wff1_ref, wff2_ref, wout_ref,
                   o_ref, wff1_o, wff2_o, wout_o, hn_ref):
    j = pl.program_id(1)

    @pl.when(j == 0)
    def _():
        parts = [r[...].reshape(ROWS, X_PART) for r in (x0_ref, x1_ref, x2_ref, x3_ref)]
        ssq = sum(jnp.sum(p * p, axis=-1, keepdims=True) for p in parts)
        scale = lax.rsqrt(ssq * (1.0 / D_MODEL) + EPS)
        for q, p in enumerate(parts):
            cols = slice(q * X_PART, (q + 1) * X_PART)
            hn_ref[:, cols] = (p * scale * gain_ref[:, cols]).astype(BF16)

    o_ref[...] = jnp.dot(hn_ref[...], w_ref[j], preferred_element_type=F32).astype(BF16)
    wff1_o[...] = wff1_ref[...].astype(BF16)
    wff2_o[...] = wff2_ref[...].astype(BF16)
    wout_o[...] = wout_ref[...].astype(BF16)


def _inproj(x, gain, w_in_tiles, w_ff1, w_ff2, w_out):
    def x_part(q):
        return pl.BlockSpec(
            (BATCH, TIME_TILE, X_PART),
            lambda i, j: (0, jnp.minimum(i + jnp.where(j > q, 1, 0), N_TILES - 1), q))

    def cast_rows(rows, width):
        return pl.BlockSpec((rows // N_CAST_STEPS, width),
                            lambda i, j: (i * N_PROJ_TILES + j, 0))

    return pl.pallas_call(
        _inproj_kernel,
        grid=(N_TILES, N_PROJ_TILES),
        in_specs=[
            x_part(0), x_part(1), x_part(2), x_part(3),
            pl.BlockSpec((1, D_MODEL), lambda i, j: (0, 0)),
            pl.BlockSpec((N_PROJ_TILES, D_MODEL, PROJ_TILE), lambda i, j: (0, 0, 0),
                         pipeline_mode=pl.Buffered(1)),
            cast_rows(D_MODEL, D_FF), cast_rows(D_FF, D_MODEL), cast_rows(D_MODEL, D_MODEL),
        ],
        out_specs=[
            pl.BlockSpec((None, ROWS, PROJ_TILE), lambda i, j: (i, 0, j)),
            cast_rows(D_MODEL, D_FF), cast_rows(D_FF, D_MODEL), cast_rows(D_MODEL, D_MODEL),
        ],
        out_shape=[
            jax.ShapeDtypeStruct((N_TILES, ROWS, D_IN), BF16),
            jax.ShapeDtypeStruct((D_MODEL, D_FF), BF16),
            jax.ShapeDtypeStruct((D_FF, D_MODEL), BF16),
            jax.ShapeDtypeStruct((D_MODEL, D_MODEL), BF16),
        ],
        scratch_shapes=[pltpu.VMEM((ROWS, D_MODEL), BF16)],
        compiler_params=pltpu.CompilerParams(
            dimension_semantics=("arbitrary", "arbitrary"),
            vmem_limit_bytes=VMEM_LIMIT),
        name="inproj",
    )(x, x, x, x, gain, w_in_tiles, w_ff1, w_ff2, w_out)


def _sgu_steps(b, ub_ref, vb_ref, vgain_ref, w_heads, bs_ref, gain_b_ref, o_ref):
    rows = pl.ds(b * TIME_TILE, TIME_TILE)
    v = _rms(_gelu(vb_ref[rows, :].astype(F32)), vgain_ref[...]).astype(BF16)
    yield
    pieces = []
    for h in range(SGU_HEADS):
        cols = slice(h * SGU_HEAD_DIM, (h + 1) * SGU_HEAD_DIM)
        mixed = jnp.dot(w_heads[h], v[:, cols], preferred_element_type=F32) + bs_ref[h]
        pieces.append(_gelu(ub_ref[rows, cols].astype(F32)) * mixed)
        yield
    yb = jnp.concatenate(pieces, axis=-1)
    o_ref[rows, D_SSM:D_SSM + D_SGU] = _rms(yb, gain_b_ref[...]).astype(BF16)
    yield


def _advance(gens, n):
    while n > 0 and gens:
        try:
            next(gens[0])
            n -= 1
        except StopIteration:
            gens.pop(0)


def _mixer_kernel(ua_ref, ga_ref, ub_ref, vb_ref, bd_ref, cd_ref, are_ref, aim_ref, dskip_ref,
                  vgain_ref, w_ref, bs_ref, gain_a_ref, gain_b_ref, o_ref,
                  state_ref, bu_ref, utb_ref, y_ref):
    @pl.when(pl.program_id(0) == 0)
    def _():
        state_ref[...] = jnp.zeros_like(state_ref)

    row_chunk = lax.broadcasted_iota(jnp.int32, (SGU_BLOCK, SGU_BLOCK), 0) // CHUNK
    col_chunk = lax.broadcasted_iota(jnp.int32, (SGU_BLOCK, SGU_BLOCK), 1) // CHUNK
    causal = col_chunk <= row_chunk
    w_heads = [jnp.where(causal, w_ref[h], 0.0).astype(BF16) for h in range(SGU_HEADS)]

    for b in range(BATCH):
        u_b = ua_ref[pl.ds(b * TIME_TILE, TIME_TILE), :].astype(F32)
        for s in range(D_SSM // 128):
            utb_ref[s, pl.ds(b, TIME_TILE, stride=BATCH), :] = u_b[:, s * 128:(s + 1) * 128]

    half = CHUNK_ST // 128

    def gating(batches):
        return [_sgu_steps(b, ub_ref, vb_ref, vgain_ref, w_heads, bs_ref, gain_b_ref, o_ref)
                for b in batches]

    def input_matmul(kb, side):
        u_tb = jnp.concatenate([utb_ref[2 * kb], utb_ref[2 * kb + 1]], axis=1)
        u_tb16 = u_tb.astype(BF16)
        for jp in range(2 * CHUNK_ST // 256):
            cols = slice(jp * 256, (jp + 1) * 256)
            bu_ref[:, cols] = jnp.dot(u_tb16, bd_ref[kb, :, cols], preferred_element_type=F32)
            _advance(side, 2)
        return u_tb

    side = gating(GATING_PLAN[0])
    u_tb = input_matmul(0, side)
    _advance(side, 100)
    for kb in range(N_CHUNKS):
        a_re = [are_ref[kb, :, j * 128:(j + 1) * 128] for j in range(half)]
        a_im = [aim_ref[kb, :, j * 128:(j + 1) * 128] for j in range(half)]

        def step(t, carry):
            s_re, s_im = carry
            at_t = pl.ds(pl.multiple_of(t * BATCH, BATCH), BATCH)
            b_re = [bu_ref[at_t, j * 128:(j + 1) * 128] for j in range(half)]
            b_im = [bu_ref[at_t, CHUNK_ST + j * 128:CHUNK_ST + (j + 1) * 128]
                    for j in range(half)]
            n_re = tuple(a_re[j] * s_re[j] - a_im[j] * s_im[j] + b_re[j] for j in range(half))
            n_im = tuple(a_re[j] * s_im[j] + a_im[j] * s_re[j] + b_im[j] for j in range(half))
            for j in range(half):
                bu_ref[at_t, j * 128:(j + 1) * 128] = n_re[j]
                bu_ref[at_t, CHUNK_ST + j * 128:CHUNK_ST + (j + 1) * 128] = n_im[j]
            return n_re, n_im

        s0_re = tuple(state_ref[kb, :, j * 128:(j + 1) * 128] for j in range(half))
        s0_im = tuple(state_ref[kb, :, CHUNK_ST + j * 128:CHUNK_ST + (j + 1) * 128]
                      for j in range(half))
        s_re, s_im = lax.fori_loop(0, TIME_TILE, step, (s0_re, s0_im), unroll=SCAN_UNROLL)
        for j in range(half):
            state_ref[kb, :, j * 128:(j + 1) * 128] = s_re[j]
            state_ref[kb, :, CHUNK_ST + j * 128:CHUNK_ST + (j + 1) * 128] = s_im[j]

        side = gating(GATING_PLAN[kb + 1])
        cols = slice(kb * CHUNK_IN, (kb + 1) * CHUNK_IN)
        y_re = jnp.dot(bu_ref[:, 0:CHUNK_ST].astype(BF16), cd_ref[kb, 0:CHUNK_ST, :],
                       preferred_element_type=F32)
        y_im = jnp.dot(bu_ref[:, CHUNK_ST:2 * CHUNK_ST].astype(BF16),
                       cd_ref[kb, CHUNK_ST:2 * CHUNK_ST, :], preferred_element_type=F32)
        _advance(side, 5)
        y_tb = (y_re + dskip_ref[:, cols] * u_tb) - y_im
        utb_ref[2 * kb] = y_tb[:, 0:128]
        utb_ref[2 * kb + 1] = y_tb[:, 128:256]
        if kb + 1 < N_CHUNKS:
            u_tb = input_matmul(kb + 1, side)
        for b in range(BATCH):
            rows = pl.ds(b * TIME_TILE, TIME_TILE)
            y = jnp.concatenate(
                [utb_ref[2 * kb + s, pl.ds(b, TIME_TILE, stride=BATCH), :] for s in range(2)],
                axis=1)
            y_ref[rows, cols] = _gelu(y) * _sigmoid(ga_ref[rows, cols].astype(F32))
        _advance(side, 100)

    for b in range(BATCH):
        rows = pl.ds(b * TIME_TILE, TIME_TILE)
        o_ref[rows, 0:D_SSM] = _rms(y_ref[rows, :], gain_a_ref[...]).astype(BF16)


def _mixer(proj, bd, cd, a_re, a_im, d_skip, v_gain, w_s, bs, gain_a, gain_b):
    def const(shape):
        nd = len(shape)
        return pl.BlockSpec(shape, lambda i: (0,) * nd, pipeline_mode=pl.Buffered(1))

    def col(j):
        return pl.BlockSpec((None, ROWS, D_SSM), lambda i: (i, 0, j))

    return pl.pallas_call(
        _mixer_kernel,
        grid=(N_TILES,),
        in_specs=[
            col(0), col(1), col(2), col(3),
            const((N_CHUNKS, CHUNK_IN, 2 * CHUNK_ST)),
            const((N_CHUNKS, 2 * CHUNK_ST, CHUNK_IN)),
            const((N_CHUNKS, BATCH, CHUNK_ST)), const((N_CHUNKS, BATCH, CHUNK_ST)),
            const((1, D_SSM)), const((1, D_SGU)),
            const((SGU_HEADS, SGU_BLOCK, SGU_BLOCK)),
            const((SGU_HEADS, SGU_BLOCK, SGU_HEAD_DIM)),
            const((1, D_SSM)), const((1, D_SGU)),
        ],
        out_specs=pl.BlockSpec((None, ROWS, D_MODEL), lambda i: (i, 0, 0)),
        out_shape=jax.ShapeDtypeStruct((N_TILES, ROWS, D_MODEL), BF16),
        scratch_shapes=[
            pltpu.VMEM((N_CHUNKS, BATCH, 2 * CHUNK_ST), F32),
            pltpu.VMEM((ROWS, 2 * CHUNK_ST), F32),
            pltpu.VMEM((D_SSM // 128, ROWS, 128), F32),
            pltpu.VMEM((ROWS, D_SSM), F32),
        ],
        compiler_params=pltpu.CompilerParams(
            dimension_semantics=("arbitrary",),
            vmem_limit_bytes=VMEM_LIMIT),
        name="mixer",
    )(proj, proj, proj, proj, bd, cd, a_re, a_im, d_skip, v_gain, w_s, bs, gain_a, gain_b)


def _outproj_kernel(m_ref, w_ref, x_ref, o_ref):
    for c in range(D_MODEL // PROJ_TILE):
        cols = slice(c * PROJ_TILE, (c + 1) * PROJ_TILE)
        acc = jnp.dot(m_ref[...], w_ref[:, cols], preferred_element_type=F32)
        o_ref[:, cols] = x_ref[:, :, cols].reshape(ROWS, PROJ_TILE) + acc


def _outproj(mixed, w_out, x):
    return pl.pallas_call(
        _outproj_kernel,
        grid=(N_TILES,),
        in_specs=[
            pl.BlockSpec((None, ROWS, D_MODEL), lambda i: (i, 0, 0)),
            pl.BlockSpec((D_MODEL, D_MODEL), lambda i: (0, 0), pipeline_mode=pl.Buffered(1)),
            pl.BlockSpec((BATCH, TIME_TILE, D_MODEL), lambda i: (0, i, 0)),
        ],
        out_specs=pl.BlockSpec((None, ROWS, D_MODEL), lambda i: (i, 0, 0)),
        out_shape=jax.ShapeDtypeStruct((N_TILES, ROWS, D_MODEL), F32),
        compiler_params=pltpu.CompilerParams(
            dimension_semantics=("arbitrary",),
            vmem_limit_bytes=VMEM_LIMIT),
        name="outproj",
    )(mixed, w_out, x)


def _ffn_kernel(h0_ref, h1_ref, gain_ref, w1_ref, w2_ref, fgain_ref, o_ref, hn_ref):
    k = pl.program_id(1)

    @pl.when(k == 0)
    def _():
        parts = [r[...] for r in (h0_ref, h1_ref)]
        ssq = sum(jnp.sum(p * p, axis=-1, keepdims=True) for p in parts)
        scale = lax.rsqrt(ssq * (1.0 / D_MODEL) + EPS)
        for q, p in enumerate(parts):
            cols = slice(q * H_PART, (q + 1) * H_PART)
            hn_ref[:, cols] = (p * scale * gain_ref[:, cols]).astype(BF16)
            o_ref[:, :, cols] = p.reshape(BATCH, TIME_TILE, H_PART)

    a = jnp.dot(hn_ref[...], w1_ref[...], preferred_element_type=F32)
    a = jnp.maximum(a, 0.0)
    a = (a * a).astype(BF16)
    for c in range(D_MODEL // FF_TILE):
        cols = slice(c * FF_TILE, (c + 1) * FF_TILE)
        upd = jnp.dot(a, w2_ref[:, cols], preferred_element_type=F32)
        o_ref[:, :, cols] += upd.reshape(BATCH, TIME_TILE, FF_TILE)

    @pl.when(k == pl.num_programs(1) - 1)
    def _():
        acc = o_ref[...].reshape(ROWS, D_MODEL)
        o_ref[...] = _rms(acc, fgain_ref[...]).reshape(BATCH, TIME_TILE, D_MODEL)


def _ffn(h, gain, w1, w2, fgain):
    def h_part(q):
        return pl.BlockSpec(
            (None, ROWS, H_PART),
            lambda i, k: (jnp.minimum(i + jnp.where(k > H_PART_SWITCH[q], 1, 0), N_TILES - 1),
                          0, q))

    return pl.pallas_call(
        _ffn_kernel,
        grid=(N_TILES, D_FF // FF_TILE),
        in_specs=[
            h_part(0), h_part(1),
            pl.BlockSpec((1, D_MODEL), lambda i, k: (0, 0)),
            pl.BlockSpec((D_MODEL, FF_TILE), lambda i, k: (0, k)),
            pl.BlockSpec((FF_TILE, D_MODEL), lambda i, k: (k, 0)),
            pl.BlockSpec((1, D_MODEL), lambda i, k: (0, 0)),
        ],
        out_specs=pl.BlockSpec((BATCH, TIME_TILE, D_MODEL), lambda i, k: (0, i, 0)),
        out_shape=jax.ShapeDtypeStruct((BATCH, SEQ, D_MODEL), F32),
        scratch_shapes=[pltpu.VMEM((ROWS, D_MODEL), BF16)],
        compiler_params=pltpu.CompilerParams(
            dimension_semantics=("arbitrary", "arbitrary"),
            vmem_limit_bytes=VMEM_LIMIT),
        name="ffn",
    )(h, h, gain, w1, w2, fgain)


def _ssm_params(a_re, a_im, log_dt, b_re, b_im, c_re, c_im):
    dt = jnp.exp(log_dt.astype(F32))[:, None]
    lam_re = jnp.minimum(a_re.astype(F32), -1e-4)
    lam_im = a_im.astype(F32)
    decay = jnp.exp(lam_re * dt)
    abar_re = decay * jnp.cos(lam_im * dt)
    abar_im = decay * jnp.sin(lam_im * dt)
    den = lam_re * lam_re + lam_im * lam_im
    num_re = abar_re - 1.0
    coef_re = (num_re * lam_re + abar_im * lam_im) / den
    coef_im = (abar_im * lam_re - num_re * lam_im) / den
    br = b_re.astype(F32)
    bi = b_im.astype(F32)
    bbar_re = coef_re[..., None] * br - coef_im[..., None] * bi
    bbar_im = coef_re[..., None] * bi + coef_im[..., None] * br

    g = GROUPS_PER_CHUNK

    def chunked(m):
        return jnp.swapaxes(m.reshape(N_CHUNKS, g, m.shape[1], m.shape[2]), 2, 3)

    b_parts = jnp.stack([chunked(bbar_re), chunked(bbar_im)], axis=3)
    b_rows = b_parts.reshape(N_CHUNKS, CHUNK_IN, 2, 1, SSM_STATE)
    b_tiled = jnp.broadcast_to(b_rows, (N_CHUNKS, CHUNK_IN, 2, g, SSM_STATE))
    row_group = lax.broadcasted_iota(jnp.int32, b_tiled.shape, 1) // SSM_GROUP
    col_group = lax.broadcasted_iota(jnp.int32, b_tiled.shape, 3)
    bd = jnp.where(row_group == col_group, b_tiled, 0.0)
    bd = bd.reshape(N_CHUNKS, CHUNK_IN, 2 * CHUNK_ST).astype(BF16)

    c_parts = jnp.stack([chunked(c_re.astype(F32)), chunked(c_im.astype(F32))], axis=1)
    c_rows = c_parts.reshape(N_CHUNKS, 2 * CHUNK_ST, 1, SSM_GROUP)
    c_tiled = jnp.broadcast_to(c_rows, (N_CHUNKS, 2 * CHUNK_ST, g, SSM_GROUP))
    row_group = (lax.broadcasted_iota(jnp.int32, c_tiled.shape, 1) % CHUNK_ST) // SSM_STATE
    col_group = lax.broadcasted_iota(jnp.int32, c_tiled.shape, 2)
    cd = jnp.where(row_group == col_group, c_tiled, 0.0)
    cd = cd.reshape(N_CHUNKS, 2 * CHUNK_ST, CHUNK_IN).astype(BF16)

    def per_chunk(a):
        a = a.reshape(N_CHUNKS, 1, CHUNK_ST)
        return jnp.broadcast_to(a, (N_CHUNKS, BATCH, CHUNK_ST))

    return bd, cd, per_chunk(abar_re), per_chunk(abar_im)


def kernel(x, w_in, ssm_a_re, ssm_a_im, ssm_log_dt, ssm_b_re, ssm_b_im, ssm_c_re, ssm_c_im,
           ssm_d, sgu_v_gain, sgu_w, sgu_b, out_gain_ssm, out_gain_sgu, w_out,
           mix_norm_gain, ffn_norm_gain, w_ff1, w_ff2, final_norm_gain):
    row = lambda v: v.astype(F32).reshape(1, -1)
    assert x.shape == (BATCH, SEQ, D_MODEL) and w_in.shape[0] == 1, "single-layer block only"
    layer = 0
    bd, cd, a_re, a_im = _ssm_params(
        ssm_a_re[layer], ssm_a_im[layer], ssm_log_dt[layer], ssm_b_re[layer],
        ssm_b_im[layer], ssm_c_re[layer], ssm_c_im[layer])
    bs = jnp.broadcast_to(sgu_b[layer].astype(F32)[:, :, None],
                          (SGU_HEADS, SGU_BLOCK, SGU_HEAD_DIM))
    w_in_tiles = jnp.swapaxes(
        w_in[layer].astype(BF16).reshape(D_MODEL, N_PROJ_TILES, PROJ_TILE), 0, 1)
    proj, w_ff1_16, w_ff2_16, w_out_16 = _inproj(
        x, row(mix_norm_gain[layer]), w_in_tiles, w_ff1[layer], w_ff2[layer], w_out[layer])
    mixed = _mixer(proj, bd, cd, a_re, a_im, row(ssm_d[layer]),
                   row(sgu_v_gain[layer]), sgu_w[layer].astype(F32), bs,
                   row(out_gain_ssm[layer]), row(out_gain_sgu[layer]))
    h_tiles = _outproj(mixed, w_out_16, x)
    return _ffn(h_tiles, row(ffn_norm_gain[layer]), w_ff1_16, w_ff2_16, row(final_norm_gain))
```

```python
import math

import jax
import jax.numpy as jnp
from jax import lax
from jax.experimental import pallas as pl
from jax.experimental.pallas import tpu as pltpu

D_MODEL = 2048
BATCH = 8
SEQ = 2048
CHUNK = 64
D_SSM = 1024
D_SGU = 1024
SSM_GROUP = 16
SSM_GROUPS = 64
SSM_STATE = 64
SGU_BLOCK = 128
SGU_HEADS = 8
SGU_HEAD_DIM = 128
D_FF = 4 * D_MODEL
D_IN = 2 * D_SSM + 2 * D_SGU
EPS = 1e-5

TIME_TILE = SGU_BLOCK
ROWS = BATCH * TIME_TILE
N_TILES = SEQ // TIME_TILE
GROUPS_PER_CHUNK = 16
N_CHUNKS = SSM_GROUPS // GROUPS_PER_CHUNK
CHUNK_IN = GROUPS_PER_CHUNK * SSM_GROUP
CHUNK_ST = GROUPS_PER_CHUNK * SSM_STATE
GATING_PLAN = ((0,), (1, 2), (3, 4), (5, 6), (7,))
SCAN_UNROLL = 8
FF_TILE = 512
PROJ_TILE = 1024
N_PROJ_TILES = D_IN // PROJ_TILE
X_PART = D_MODEL // N_PROJ_TILES
N_CAST_STEPS = N_TILES * N_PROJ_TILES
VMEM_LIMIT = 56 * 1024 * 1024

F32 = jnp.float32
BF16 = jnp.bfloat16


def _rms(x, gain):
    return x * lax.rsqrt(jnp.mean(x * x, axis=-1, keepdims=True) + EPS) * gain


def _gelu(x):
    c = math.sqrt(2.0 / math.pi)
    half_x = 0.5 * x
    return half_x + half_x * jnp.tanh(x * (c + (0.044715 * c) * (x * x)))


def _sigmoid(x):
    return 1.0 / (1.0 + jnp.exp(-x))


def _inproj_kernel(x0_ref, x1_ref, x2_ref, x3_ref, gain_ref, w_ref, wff1_ref, wff2_ref, wout_ref,
                   o_ref, wff1_o, wff2_o, wout_o, hn_ref):
    j = pl.program_id(1)

    @pl.when(j == 0)
    def _():
        parts = [r[...].reshape(ROWS, X_PART) for r in (x0_ref, x1_ref, x2_ref, x3_ref)]
        ssq = sum(jnp.sum(p * p, axis=-1, keepdims=True) for p in parts)
        scale = lax.rsqrt(ssq * (1.0 / D_MODEL) + EPS)
        for q, p in enumerate(parts):
            cols = slice(q * X_PART, (q + 1) * X_PART)
            hn_ref[:, cols] = (p * scale * gain_ref[:, cols]).astype(BF16)

    w_cols = pl.ds(pl.multiple_of(j * PROJ_TILE, PROJ_TILE), PROJ_TILE)
    o_ref[...] = jnp.dot(hn_ref[...], w_ref[:, w_cols], preferred_element_type=F32).astype(BF16)
    wff1_o[...] = wff1_ref[...].astype(BF16)
    wff2_o[...] = wff2_ref[...].astype(BF16)
    wout_o[...] = wout_ref[...].astype(BF16)


def _inproj(x, gain, w_in_tiles, w_ff1, w_ff2, w_out):
    def x_part(q):
        return pl.BlockSpec(
            (BATCH, TIME_TILE, X_PART),
            lambda i, j: (0, jnp.minimum(i + jnp.where(j > q, 1, 0), N_TILES - 1), q))

    def cast_rows(rows, width):
        return pl.BlockSpec((rows // N_CAST_STEPS, width),
                            lambda i, j: (i * N_PROJ_TILES + j, 0))

    return pl.pallas_call(
        _inproj_kernel,
        grid=(N_TILES, N_PROJ_TILES),
        in_specs=[
            x_part(0), x_part(1), x_part(2), x_part(3),
            pl.BlockSpec((1, D_MODEL), lambda i, j: (0, 0)),
            pl.BlockSpec((D_MODEL, D_IN), lambda i, j: (0, 0), pipeline_mode=pl.Buffered(1)),
            cast_rows(D_MODEL, D_FF), cast_rows(D_FF, D_MODEL), cast_rows(D_MODEL, D_MODEL),
        ],
        out_specs=[
            pl.BlockSpec((None, ROWS, PROJ_TILE), lambda i, j: (i, 0, j)),
            cast_rows(D_MODEL, D_FF), cast_rows(D_FF, D_MODEL), cast_rows(D_MODEL, D_MODEL),
        ],
        out_shape=[
            jax.ShapeDtypeStruct((N_TILES, ROWS, D_IN), BF16),
            jax.ShapeDtypeStruct((D_MODEL, D_FF), BF16),
            jax.ShapeDtypeStruct((D_FF, D_MODEL), BF16),
            jax.ShapeDtypeStruct((D_MODEL, D_MODEL), BF16),
        ],
        scratch_shapes=[pltpu.VMEM((ROWS, D_MODEL), BF16)],
        compiler_params=pltpu.CompilerParams(
            dimension_semantics=("arbitrary", "arbitrary"),
            vmem_limit_bytes=VMEM_LIMIT),
        name="inproj",
    )(x, x, x, x, gain, w_in_tiles, w_ff1, w_ff2, w_out)


def _sgu_steps(b, ub_ref, vb_ref, vgain_ref, w_heads, bs_ref, gain_b_ref, o_ref):
    rows = pl.ds(b * TIME_TILE, TIME_TILE)
    v = _rms(_gelu(vb_ref[rows, :].astype(F32)), vgain_ref[...]).astype(BF16)
    yield
    pieces = []
    for h in range(SGU_HEADS):
        cols = slice(h * SGU_HEAD_DIM, (h + 1) * SGU_HEAD_DIM)
        mixed = jnp.dot(w_heads[h], v[:, cols], preferred_element_type=F32) + bs_ref[h]
        pieces.append(_gelu(ub_ref[rows, cols].astype(F32)) * mixed)
        yield
    yb = jnp.concatenate(pieces, axis=-1)
    o_ref[rows, D_SSM:D_SSM + D_SGU] = _rms(yb, gain_b_ref[...]).astype(BF16)
    yield


def _advance(gens, n):
    while n > 0 and gens:
        try:
            next(gens[0])
            n -= 1
        except StopIteration:
            gens.pop(0)


def _mixer_kernel(ua_ref, ga_ref, ub_ref, vb_ref, bd_ref, cd_ref, are_ref, aim_ref, dskip_ref,
                  vgain_ref, w_ref, bs_ref, gain_a_ref, gain_b_ref, o_ref,
                  state_ref, bu_ref, utb_ref, y_ref):
    @pl.when(pl.program_id(0) == 0)
    def _():
        state_ref[...] = jnp.zeros_like(state_ref)

    row_chunk = lax.broadcasted_iota(jnp.int32, (SGU_BLOCK, SGU_BLOCK), 0) // CHUNK
    col_chunk = lax.broadcasted_iota(jnp.int32, (SGU_BLOCK, SGU_BLOCK), 1) // CHUNK
    causal = col_chunk <= row_chunk
    w_heads = [jnp.where(causal, w_ref[h], 0.0).astype(BF16) for h in range(SGU_HEADS)]

    for b in range(BATCH):
        u_b = ua_ref[pl.ds(b * TIME_TILE, TIME_TILE), :].astype(F32)
        for s in range(D_SSM // 128):
            utb_ref[s, pl.ds(b, TIME_TILE, stride=BATCH), :] = u_b[:, s * 128:(s + 1) * 128]

    half = CHUNK_ST // 128

    def gating(batches):
        return [_sgu_steps(b, ub_ref, vb_ref, vgain_ref, w_heads, bs_ref, gain_b_ref, o_ref)
                for b in batches]

    def input_matmul(kb, side):
        u_tb = jnp.concatenate([utb_ref[2 * kb], utb_ref[2 * kb + 1]], axis=1)
        u_tb16 = u_tb.astype(BF16)
        for jp in range(2 * CHUNK_ST // 256):
            cols = slice(jp * 256, (jp + 1) * 256)
            bu_ref[:, cols] = jnp.dot(u_tb16, bd_ref[kb, :, cols], preferred_element_type=F32)
            _advance(side, 2)
        return u_tb

    side = gating(GATING_PLAN[0])
    u_tb = input_matmul(0, side)
    _advance(side, 100)
    for kb in range(N_CHUNKS):
        a_re = [are_ref[kb, :, j * 128:(j + 1) * 128] for j in range(half)]
        a_im = [aim_ref[kb, :, j * 128:(j + 1) * 128] for j in range(half)]

        def step(t, carry):
            s_re, s_im = carry
            at_t = pl.ds(pl.multiple_of(t * BATCH, BATCH), BATCH)
            b_re = [bu_ref[at_t, j * 128:(j + 1) * 128] for j in range(half)]
            b_im = [bu_ref[at_t, CHUNK_ST + j * 128:CHUNK_ST + (j + 1) * 128]
                    for j in range(half)]
            n_re = tuple(a_re[j] * s_re[j] - a_im[j] * s_im[j] + b_re[j] for j in range(half))
            n_im = tuple(a_re[j] * s_im[j] + a_im[j] * s_re[j] + b_im[j] for j in range(half))
            for j in range(half):
                bu_ref[at_t, j * 128:(j + 1) * 128] = n_re[j]
                bu_ref[at_t, CHUNK_ST + j * 128:CHUNK_ST + (j + 1) * 128] = n_im[j]
            return n_re, n_im

        s0_re = tuple(state_ref[kb, :, j * 128:(j + 1) * 128] for j in range(half))
        s0_im = tuple(state_ref[kb, :, CHUNK_ST + j * 128:CHUNK_ST + (j + 1) * 128]
                      for j in range(half))
        s_re, s_im = lax.fori_loop(0, TIME_TILE, step, (s0_re, s0_im), unroll=SCAN_UNROLL)
        for j in range(half):
            state_ref[kb, :, j * 128:(j + 1) * 128] = s_re[j]
            state_ref[kb, :, CHUNK_ST + j * 128:CHUNK_ST + (j + 1) * 128] = s_im[j]

        side = gating(GATING_PLAN[kb + 1])
        cols = slice(kb * CHUNK_IN, (kb + 1) * CHUNK_IN)
        hk = CHUNK_ST // 2

        def part(lo):
            return jnp.dot(bu_ref[:, lo:lo + hk].astype(BF16), cd_ref[kb, lo:lo + hk, :],
                           preferred_element_type=F32)

        d0 = part(0) - part(CHUNK_ST)
        _advance(side, 3)
        d1 = part(hk) - part(CHUNK_ST + hk)
        _advance(side, 2)
        y_tb = (d0 + dskip_ref[:, cols] * u_tb) + d1
        utb_ref[2 * kb] = y_tb[:, 0:128]
        utb_ref[2 * kb + 1] = y_tb[:, 128:256]
        if kb + 1 < N_CHUNKS:
            u_tb = input_matmul(kb + 1, side)
        for b in range(BATCH):
            rows = pl.ds(b * TIME_TILE, TIME_TILE)
            y = jnp.concatenate(
                [utb_ref[2 * kb + s, pl.ds(b, TIME_TILE, stride=BATCH), :] for s in range(2)],
                axis=1)
            y_ref[rows, cols] = _gelu(y) * _sigmoid(ga_ref[rows, cols].astype(F32))
        _advance(side, 100)

    for b in range(BATCH):
        rows = pl.ds(b * TIME_TILE, TIME_TILE)
        o_ref[rows, 0:D_SSM] = _rms(y_ref[rows, :], gain_a_ref[...]).astype(BF16)


def _mixer(proj, bd, cd, a_re, a_im, d_skip, v_gain, w_s, bs, gain_a, gain_b):
    def const(shape):
        nd = len(shape)
        return pl.BlockSpec(shape, lambda i: (0,) * nd, pipeline_mode=pl.Buffered(1))

    def col(j):
        return pl.BlockSpec((None, ROWS, D_SSM), lambda i: (i, 0, j))

    return pl.pallas_call(
        _mixer_kernel,
        grid=(N_TILES,),
        in_specs=[
            col(0), col(1), col(2), col(3),
            const((N_CHUNKS, CHUNK_IN, 2 * CHUNK_ST)),
            const((N_CHUNKS, 2 * CHUNK_ST, CHUNK_IN)),
            const((N_CHUNKS, BATCH, CHUNK_ST)), const((N_CHUNKS, BATCH, CHUNK_ST)),
            const((1, D_SSM)), const((1, D_SGU)),
            const((SGU_HEADS, SGU_BLOCK, SGU_BLOCK)),
            const((SGU_HEADS, SGU_BLOCK, SGU_HEAD_DIM)),
            const((1, D_SSM)), const((1, D_SGU)),
        ],
        out_specs=pl.BlockSpec((None, ROWS, D_MODEL), lambda i: (i, 0, 0)),
        out_shape=jax.ShapeDtypeStruct((N_TILES, ROWS, D_MODEL), BF16),
        scratch_shapes=[
            pltpu.VMEM((N_CHUNKS, BATCH, 2 * CHUNK_ST), F32),
            pltpu.VMEM((ROWS, 2 * CHUNK_ST), F32),
            pltpu.VMEM((D_SSM // 128, ROWS, 128), F32),
            pltpu.VMEM((ROWS, D_SSM), F32),
        ],
        compiler_params=pltpu.CompilerParams(
            dimension_semantics=("arbitrary",),
            vmem_limit_bytes=VMEM_LIMIT),
        name="mixer",
    )(proj, proj, proj, proj, bd, cd, a_re, a_im, d_skip, v_gain, w_s, bs, gain_a, gain_b)


def _outproj_kernel(m_ref, w_ref, x_ref, o_ref):
    for c in range(D_MODEL // PROJ_TILE):
        cols = slice(c * PROJ_TILE, (c + 1) * PROJ_TILE)
        acc = jnp.dot(m_ref[...], w_ref[:, cols], preferred_element_type=F32)
        o_ref[:, cols] = x_ref[:, :, cols].reshape(ROWS, PROJ_TILE) + acc


def _outproj(mixed, w_out, x):
    return pl.pallas_call(
        _outproj_kernel,
        grid=(N_TILES,),
        in_specs=[
            pl.BlockSpec((None, ROWS, D_MODEL), lambda i: (i, 0, 0)),
            pl.BlockSpec((D_MODEL, D_MODEL), lambda i: (0, 0), pipeline_mode=pl.Buffered(1)),
            pl.BlockSpec((BATCH, TIME_TILE, D_MODEL), lambda i: (0, i, 0)),
        ],
        out_specs=pl.BlockSpec((None, ROWS, D_MODEL), lambda i: (i, 0, 0)),
        out_shape=jax.ShapeDtypeStruct((N_TILES, ROWS, D_MODEL), F32),
        compiler_params=pltpu.CompilerParams(
            dimension_semantics=("arbitrary",),
            vmem_limit_bytes=VMEM_LIMIT),
        name="outproj",
    )(mixed, w_out, x)


def _ffn_kernel(h_ref, gain_ref, w1_ref, w2_ref, fgain_ref, o_ref, hn_ref):
    k = pl.program_id(1)

    @pl.when(k == 0)
    def _():
        h = h_ref[...]
        hn_ref[...] = _rms(h, gain_ref[...]).astype(BF16)
        o_ref[...] = h.reshape(BATCH, TIME_TILE, D_MODEL)

    a = jnp.dot(hn_ref[...], w1_ref[...], preferred_element_type=F32)
    a = jnp.maximum(a, 0.0)
    a = (a * a).astype(BF16)
    for c in range(D_MODEL // FF_TILE):
        cols = slice(c * FF_TILE, (c + 1) * FF_TILE)
        upd = jnp.dot(a, w2_ref[:, cols], preferred_element_type=F32)
        o_ref[:, :, cols] += upd.reshape(BATCH, TIME_TILE, FF_TILE)

    @pl.when(k == pl.num_programs(1) - 1)
    def _():
        acc = o_ref[...].reshape(ROWS, D_MODEL)
        o_ref[...] = _rms(acc, fgain_ref[...]).reshape(BATCH, TIME_TILE, D_MODEL)


def _ffn(h, gain, w1, w2, fgain):
    return pl.pallas_call(
        _ffn_kernel,
        grid=(N_TILES, D_FF // FF_TILE),
        in_specs=[
            pl.BlockSpec((None, ROWS, D_MODEL), lambda i, k: (i, 0, 0)),
            pl.BlockSpec((1, D_MODEL), lambda i, k: (0, 0)),
            pl.BlockSpec((D_MODEL, FF_TILE), lambda i, k: (0, k)),
            pl.BlockSpec((FF_TILE, D_MODEL), lambda i, k: (k, 0)),
            pl.BlockSpec((1, D_MODEL), lambda i, k: (0, 0)),
        ],
        out_specs=pl.BlockSpec((BATCH, TIME_TILE, D_MODEL), lambda i, k: (0, i, 0)),
        out_shape=jax.ShapeDtypeStruct((BATCH, SEQ, D_MODEL), F32),
        scratch_shapes=[pltpu.VMEM((ROWS, D_MODEL), BF16)],
        compiler_params=pltpu.CompilerParams(
            dimension_semantics=("arbitrary", "arbitrary"),
            vmem_limit_bytes=VMEM_LIMIT),
        name="ffn",
    )(h, gain, w1, w2, fgain)


def _ssm_params(a_re, a_im, log_dt, b_re, b_im, c_re, c_im):
    dt = jnp.exp(log_dt.astype(F32))[:, None]
    lam_re = jnp.minimum(a_re.astype(F32), -1e-4)
    lam_im = a_im.astype(F32)
    decay = jnp.exp(lam_re * dt)
    abar_re = decay * jnp.cos(lam_im * dt)
    abar_im = decay * jnp.sin(lam_im * dt)
    den = lam_re * lam_re + lam_im * lam_im
    num_re = abar_re - 1.0
    coef_re = (num_re * lam_re + abar_im * lam_im) / den
    coef_im = (abar_im * lam_re - num_re * lam_im) / den
    br = b_re.astype(F32)
    bi = b_im.astype(F32)
    bbar_re = coef_re[..., None] * br - coef_im[..., None] * bi
    bbar_im = coef_re[..., None] * bi + coef_im[..., None] * br

    g = GROUPS_PER_CHUNK

    def chunked(m):
        return jnp.swapaxes(m.reshape(N_CHUNKS, g, m.shape[1], m.shape[2]), 2, 3)

    b_parts = jnp.stack([chunked(bbar_re), chunked(bbar_im)], axis=3)
    b_rows = b_parts.reshape(N_CHUNKS, CHUNK_IN, 2, 1, SSM_STATE)
    b_tiled = jnp.broadcast_to(b_rows, (N_CHUNKS, CHUNK_IN, 2, g, SSM_STATE))
    row_group = lax.broadcasted_iota(jnp.int32, b_tiled.shape, 1) // SSM_GROUP
    col_group = lax.broadcasted_iota(jnp.int32, b_tiled.shape, 3)
    bd = jnp.where(row_group == col_group, b_tiled, 0.0)
    bd = bd.reshape(N_CHUNKS, CHUNK_IN, 2 * CHUNK_ST).astype(BF16)

    c_parts = jnp.stack([chunked(c_re.astype(F32)), chunked(c_im.astype(F32))], axis=1)
    c_rows = c_parts.reshape(N_CHUNKS, 2 * CHUNK_ST, 1, SSM_GROUP)
    c_tiled = jnp.broadcast_to(c_rows, (N_CHUNKS, 2 * CHUNK_ST, g, SSM_GROUP))
    row_group = (lax.broadcasted_iota(jnp.int32, c_tiled.shape, 1) % CHUNK_ST) // SSM_STATE
    col_group = lax.broadcasted_iota(jnp.int32, c_tiled.shape, 2)
    cd = jnp.where(row_group == col_group, c_tiled, 0.0)
    cd = cd.reshape(N_CHUNKS, 2 * CHUNK_ST, CHUNK_IN).astype(BF16)

    def per_chunk(a):
        a = a.reshape(N_CHUNKS, 1, CHUNK_ST)
        return jnp.broadcast_to(a, (N_CHUNKS, BATCH, CHUNK_ST))

    return bd, cd, per_chunk(abar_re), per_chunk(abar_im)


def kernel(x, w_in, ssm_a_re, ssm_a_im, ssm_log_dt, ssm_b_re, ssm_b_im, ssm_c_re, ssm_c_im,
           ssm_d, sgu_v_gain, sgu_w, sgu_b, out_gain_ssm, out_gain_sgu, w_out,
           mix_norm_gain, ffn_norm_gain, w_ff1, w_ff2, final_norm_gain):
    row = lambda v: v.astype(F32).reshape(1, -1)
    assert x.shape == (BATCH, SEQ, D_MODEL) and w_in.shape[0] == 1, "single-layer block only"
    layer = 0
    bd, cd, a_re, a_im = _ssm_params(
        ssm_a_re[layer], ssm_a_im[layer], ssm_log_dt[layer], ssm_b_re[layer],
        ssm_b_im[layer], ssm_c_re[layer], ssm_c_im[layer])
    bs = jnp.broadcast_to(sgu_b[layer].astype(F32)[:, :, None],
                          (SGU_HEADS, SGU_BLOCK, SGU_HEAD_DIM))
    proj, w_ff1_16, w_ff2_16, w_out_16 = _inproj(
        x, row(mix_norm_gain[layer]), w_in[layer].astype(BF16), w_ff1[layer], w_ff2[layer],
        w_out[layer])
    mixed = _mixer(proj, bd, cd, a_re, a_im, row(ssm_d[layer]),
                   row(sgu_v_gain[layer]), sgu_w[layer].astype(F32), bs,
                   row(out_gain_ssm[layer]), row(out_gain_sgu[layer]))
    h_tiles = _outproj(mixed, w_out_16, x)
    return _ffn(h_tiles, row(ffn_norm_gain[layer]), w_ff1_16, w_ff2_16, row(final_norm_gain))
```

```python
import math

import jax
import jax.numpy as jnp
from jax import lax
from jax.experimental import pallas as pl
from jax.experimental.pallas import tpu as pltpu

D_MODEL = 2048
BATCH = 8
SEQ = 2048
CHUNK = 64
D_SSM = 1024
D_SGU = 1024
SSM_GROUP = 16
SSM_GROUPS = 64
SSM_STATE = 64
SGU_BLOCK = 128
SGU_HEADS = 8
SGU_HEAD_DIM = 128
D_FF = 4 * D_MODEL
D_IN = 2 * D_SSM + 2 * D_SGU
EPS = 1e-5

TIME_TILE = SGU_BLOCK
ROWS = BATCH * TIME_TILE
N_TILES = SEQ // TIME_TILE
GROUPS_PER_CHUNK = 16
N_CHUNKS = SSM_GROUPS // GROUPS_PER_CHUNK
CHUNK_IN = GROUPS_PER_CHUNK * SSM_GROUP
CHUNK_ST = GROUPS_PER_CHUNK * SSM_STATE
GATING_PLAN = ((0,), (1, 2), (3, 4), (5, 6), (7,))
SCAN_UNROLL = 8
FF_TILE = 1024
FF_PIECE = 512
PROJ_TILE = 1024
N_PROJ_TILES = D_IN // PROJ_TILE
X_PART = D_MODEL // N_PROJ_TILES
N_CAST_STEPS = N_TILES * N_PROJ_TILES
VMEM_LIMIT = 56 * 1024 * 1024
FFN_VMEM_LIMIT = 60 * 1024 * 1024

F32 = jnp.float32
BF16 = jnp.bfloat16


def _rms(x, gain):
    return x * lax.rsqrt(jnp.mean(x * x, axis=-1, keepdims=True) + EPS) * gain


def _gelu(x):
    c = math.sqrt(2.0 / math.pi)
    half_x = 0.5 * x
    return half_x + half_x * jnp.tanh(x * (c + (0.044715 * c) * (x * x)))


def _sigmoid(x):
    return 1.0 / (1.0 + jnp.exp(-x))


def _inproj_kernel(x0_ref, x1_ref, x2_ref, x3_ref, gain_ref, w_ref, wff1_ref, wff2_ref, wout_ref,
                   o_ref, wff1_o, wff2_o, wout_o, hn_ref):
    j = pl.program_id(1)

    @pl.when(j == 0)
    def _():
        parts = [r[...].reshape(ROWS, X_PART) for r in (x0_ref, x1_ref, x2_ref, x3_ref)]
        ssq = sum(jnp.sum(p * p, axis=-1, keepdims=True) for p in parts)
        scale = lax.rsqrt(ssq * (1.0 / D_MODEL) + EPS)
        for q, p in enumerate(parts):
            cols = slice(q * X_PART, (q + 1) * X_PART)
            hn_ref[:, cols] = (p * scale * gain_ref[:, cols]).astype(BF16)

    w_cols = pl.ds(pl.multiple_of(j * PROJ_TILE, PROJ_TILE), PROJ_TILE)
    o_ref[...] = jnp.dot(hn_ref[...], w_ref[:, w_cols], preferred_element_type=F32).astype(BF16)
    wff1_o[...] = wff1_ref[...].astype(BF16)
    wff2_o[...] = wff2_ref[...].astype(BF16)
    wout_o[...] = wout_ref[...].astype(BF16)


def _inproj(x, gain, w_in_tiles, w_ff1, w_ff2, w_out):
    def x_part(q):
        return pl.BlockSpec(
            (BATCH, TIME_TILE, X_PART),
            lambda i, j: (0, jnp.minimum(i + jnp.where(j > q, 1, 0), N_TILES - 1), q))

    def cast_rows(rows, width):
        return pl.BlockSpec((rows // N_CAST_STEPS, width),
                            lambda i, j: (i * N_PROJ_TILES + j, 0))

    return pl.pallas_call(
        _inproj_kernel,
        grid=(N_TILES, N_PROJ_TILES),
        in_specs=[
            x_part(0), x_part(1), x_part(2), x_part(3),
            pl.BlockSpec((1, D_MODEL), lambda i, j: (0, 0)),
            pl.BlockSpec((D_MODEL, D_IN), lambda i, j: (0, 0), pipeline_mode=pl.Buffered(1)),
            cast_rows(D_MODEL, D_FF), cast_rows(D_FF, D_MODEL), cast_rows(D_MODEL, D_MODEL),
        ],
        out_specs=[
            pl.BlockSpec((None, ROWS, PROJ_TILE), lambda i, j: (i, 0, j)),
            cast_rows(D_MODEL, D_FF), cast_rows(D_FF, D_MODEL), cast_rows(D_MODEL, D_MODEL),
        ],
        out_shape=[
            jax.ShapeDtypeStruct((N_TILES, ROWS, D_IN), BF16),
            jax.ShapeDtypeStruct((D_MODEL, D_FF), BF16),
            jax.ShapeDtypeStruct((D_FF, D_MODEL), BF16),
            jax.ShapeDtypeStruct((D_MODEL, D_MODEL), BF16),
        ],
        scratch_shapes=[pltpu.VMEM((ROWS, D_MODEL), BF16)],
        compiler_params=pltpu.CompilerParams(
            dimension_semantics=("arbitrary", "arbitrary"),
            vmem_limit_bytes=VMEM_LIMIT),
        name="inproj",
    )(x, x, x, x, gain, w_in_tiles, w_ff1, w_ff2, w_out)


def _sgu_steps(b, ub_ref, vb_ref, vgain_ref, w_heads, bs_ref, gain_b_ref, o_ref):
    rows = pl.ds(b * TIME_TILE, TIME_TILE)
    v = _rms(_gelu(vb_ref[rows, :].astype(F32)), vgain_ref[...]).astype(BF16)
    yield
    pieces = []
    for h in range(SGU_HEADS):
        cols = slice(h * SGU_HEAD_DIM, (h + 1) * SGU_HEAD_DIM)
        mixed = jnp.dot(w_heads[h], v[:, cols], preferred_element_type=F32) + bs_ref[h]
        pieces.append(_gelu(ub_ref[rows, cols].astype(F32)) * mixed)
        yield
    yb = jnp.concatenate(pieces, axis=-1)
    o_ref[rows, D_SSM:D_SSM + D_SGU] = _rms(yb, gain_b_ref[...]).astype(BF16)
    yield


def _advance(gens, n):
    while n > 0 and gens:
        try:
            next(gens[0])
            n -= 1
        except StopIteration:
            gens.pop(0)


def _mixer_kernel(ua_ref, ga_ref, ub_ref, vb_ref, bd_ref, cd_ref, are_ref, aim_ref, dskip_ref,
                  vgain_ref, w_ref, bs_ref, gain_a_ref, gain_b_ref, o_ref,
                  state_ref, bu_ref, utb_ref, y_ref):
    @pl.when(pl.program_id(0) == 0)
    def _():
        state_ref[...] = jnp.zeros_like(state_ref)

    row_chunk = lax.broadcasted_iota(jnp.int32, (SGU_BLOCK, SGU_BLOCK), 0) // CHUNK
    col_chunk = lax.broadcasted_iota(jnp.int32, (SGU_BLOCK, SGU_BLOCK), 1) // CHUNK
    causal = col_chunk <= row_chunk
    w_heads = [jnp.where(causal, w_ref[h], 0.0).astype(BF16) for h in range(SGU_HEADS)]

    for b in range(BATCH):
        u_b = ua_ref[pl.ds(b * TIME_TILE, TIME_TILE), :].astype(F32)
        for s in range(D_SSM // 128):
            utb_ref[s, pl.ds(b, TIME_TILE, stride=BATCH), :] = u_b[:, s * 128:(s + 1) * 128]

    half = CHUNK_ST // 128

    def gating(batches):
        return [_sgu_steps(b, ub_ref, vb_ref, vgain_ref, w_heads, bs_ref, gain_b_ref, o_ref)
                for b in batches]

    def input_matmul(kb, side):
        u_tb = jnp.concatenate([utb_ref[2 * kb], utb_ref[2 * kb + 1]], axis=1)
        u_tb16 = u_tb.astype(BF16)
        for jp in range(2 * CHUNK_ST // 256):
            cols = slice(jp * 256, (jp + 1) * 256)
            bu_ref[:, cols] = jnp.dot(u_tb16, bd_ref[kb, :, cols], preferred_element_type=F32)
            _advance(side, 2)
        return u_tb

    side = gating(GATING_PLAN[0])
    u_tb = input_matmul(0, side)
    _advance(side, 100)
    for kb in range(N_CHUNKS):
        a_re = [are_ref[kb, :, j * 128:(j + 1) * 128] for j in range(half)]
        a_im = [aim_ref[kb, :, j * 128:(j + 1) * 128] for j in range(half)]

        def step(t, carry):
            s_re, s_im = carry
            at_t = pl.ds(pl.multiple_of(t * BATCH, BATCH), BATCH)
            b_re = [bu_ref[at_t, j * 128:(j + 1) * 128] for j in range(half)]
            b_im = [bu_ref[at_t, CHUNK_ST + j * 128:CHUNK_ST + (j + 1) * 128]
                    for j in range(half)]
            n_re = tuple(a_re[j] * s_re[j] - a_im[j] * s_im[j] + b_re[j] for j in range(half))
            n_im = tuple(a_re[j] * s_im[j] + a_im[j] * s_re[j] + b_im[j] for j in range(half))
            for j in range(half):
                bu_ref[at_t, j * 128:(j + 1) * 128] = n_re[j]
                bu_ref[at_t, CHUNK_ST + j * 128:CHUNK_ST + (j + 1) * 128] = n_im[j]
            return n_re, n_im

        s0_re = tuple(state_ref[kb, :, j * 128:(j + 1) * 128] for j in range(half))
        s0_im = tuple(state_ref[kb, :, CHUNK_ST + j * 128:CHUNK_ST + (j + 1) * 128]
                      for j in range(half))
        s_re, s_im = lax.fori_loop(0, TIME_TILE, step, (s0_re, s0_im), unroll=SCAN_UNROLL)
        for j in range(half):
            state_ref[kb, :, j * 128:(j + 1) * 128] = s_re[j]
            state_ref[kb, :, CHUNK_ST + j * 128:CHUNK_ST + (j + 1) * 128] = s_im[j]

        side = gating(GATING_PLAN[kb + 1])
        cols = slice(kb * CHUNK_IN, (kb + 1) * CHUNK_IN)
        hk = CHUNK_ST // 2

        def part(lo):
            return jnp.dot(bu_ref[:, lo:lo + hk].astype(BF16), cd_ref[kb, lo:lo + hk, :],
                           preferred_element_type=F32)

        d0 = part(0) - part(CHUNK_ST)
        _advance(side, 3)
        d1 = part(hk) - part(CHUNK_ST + hk)
        _advance(side, 2)
        y_tb = (d0 + dskip_ref[:, cols] * u_tb) + d1
        utb_ref[2 * kb] = y_tb[:, 0:128]
        utb_ref[2 * kb + 1] = y_tb[:, 128:256]
        if kb + 1 < N_CHUNKS:
            u_tb = input_matmul(kb + 1, side)
        for b in range(BATCH):
            rows = pl.ds(b * TIME_TILE, TIME_TILE)
            y = jnp.concatenate(
                [utb_ref[2 * kb + s, pl.ds(b, TIME_TILE, stride=BATCH), :] for s in range(2)],
                axis=1)
            y_ref[rows, cols] = _gelu(y) * _sigmoid(ga_ref[rows, cols].astype(F32))
        _advance(side, 100)

    for b in range(BATCH):
        rows = pl.ds(b * TIME_TILE, TIME_TILE)
        o_ref[rows, 0:D_SSM] = _rms(y_ref[rows, :], gain_a_ref[...]).astype(BF16)


def _mixer(proj, bd, cd, a_re, a_im, d_skip, v_gain, w_s, bs, gain_a, gain_b):
    def const(shape):
        nd = len(shape)
        return pl.BlockSpec(shape, lambda i: (0,) * nd, pipeline_mode=pl.Buffered(1))

    def col(j):
        return pl.BlockSpec((None, ROWS, D_SSM), lambda i: (i, 0, j))

    return pl.pallas_call(
        _mixer_kernel,
        grid=(N_TILES,),
        in_specs=[
            col(0), col(1), col(2), col(3),
            const((N_CHUNKS, CHUNK_IN, 2 * CHUNK_ST)),
            const((N_CHUNKS, 2 * CHUNK_ST, CHUNK_IN)),
            const((N_CHUNKS, BATCH, CHUNK_ST)), const((N_CHUNKS, BATCH, CHUNK_ST)),
            const((1, D_SSM)), const((1, D_SGU)),
            const((SGU_HEADS, SGU_BLOCK, SGU_BLOCK)),
            const((SGU_HEADS, SGU_BLOCK, SGU_HEAD_DIM)),
            const((1, D_SSM)), const((1, D_SGU)),
        ],
        out_specs=pl.BlockSpec((None, ROWS, D_MODEL), lambda i: (i, 0, 0)),
        out_shape=jax.ShapeDtypeStruct((N_TILES, ROWS, D_MODEL), BF16),
        scratch_shapes=[
            pltpu.VMEM((N_CHUNKS, BATCH, 2 * CHUNK_ST), F32),
            pltpu.VMEM((ROWS, 2 * CHUNK_ST), F32),
            pltpu.VMEM((D_SSM // 128, ROWS, 128), F32),
            pltpu.VMEM((ROWS, D_SSM), F32),
        ],
        compiler_params=pltpu.CompilerParams(
            dimension_semantics=("arbitrary",),
            vmem_limit_bytes=VMEM_LIMIT),
        name="mixer",
    )(proj, proj, proj, proj, bd, cd, a_re, a_im, d_skip, v_gain, w_s, bs, gain_a, gain_b)


def _outproj_kernel(m_ref, w_ref, x_ref, o_ref):
    for c in range(D_MODEL // PROJ_TILE):
        cols = slice(c * PROJ_TILE, (c + 1) * PROJ_TILE)
        acc = jnp.dot(m_ref[...], w_ref[:, cols], preferred_element_type=F32)
        o_ref[:, cols] = x_ref[:, :, cols].reshape(ROWS, PROJ_TILE) + acc


def _outproj(mixed, w_out, x):
    return pl.pallas_call(
        _outproj_kernel,
        grid=(N_TILES,),
        in_specs=[
            pl.BlockSpec((None, ROWS, D_MODEL), lambda i: (i, 0, 0)),
            pl.BlockSpec((D_MODEL, D_MODEL), lambda i: (0, 0), pipeline_mode=pl.Buffered(1)),
            pl.BlockSpec((BATCH, TIME_TILE, D_MODEL), lambda i: (0, i, 0)),
        ],
        out_specs=pl.BlockSpec((None, ROWS, D_MODEL), lambda i: (i, 0, 0)),
        out_shape=jax.ShapeDtypeStruct((N_TILES, ROWS, D_MODEL), F32),
        compiler_params=pltpu.CompilerParams(
            dimension_semantics=("arbitrary",),
            vmem_limit_bytes=VMEM_LIMIT),
        name="outproj",
    )(mixed, w_out, x)


def _ffn_kernel(h_ref, gain_ref, w1_ref, w2_ref, fgain_ref, o_ref, hn_ref):
    k = pl.program_id(1)

    @pl.when(k == 0)
    def _():
        h = h_ref[...]
        hn_ref[...] = _rms(h, gain_ref[...]).astype(BF16)
        o_ref[...] = h.reshape(BATCH, TIME_TILE, D_MODEL)

    acts = []
    for p in range(FF_TILE // FF_PIECE):
        z = jnp.dot(hn_ref[...], w1_ref[:, p * FF_PIECE:(p + 1) * FF_PIECE],
                    preferred_element_type=F32)
        z = jnp.maximum(z, 0.0)
        acts.append((z * z).astype(BF16))
    a = jnp.concatenate(acts, axis=1)
    for c in range(D_MODEL // FF_PIECE):
        cols = slice(c * FF_PIECE, (c + 1) * FF_PIECE)
        upd = jnp.dot(a, w2_ref[:, cols], preferred_element_type=F32)
        o_ref[:, :, cols] += upd.reshape(BATCH, TIME_TILE, FF_PIECE)

    @pl.when(k == pl.num_programs(1) - 1)
    def _():
        acc = o_ref[...].reshape(ROWS, D_MODEL)
        o_ref[...] = _rms(acc, fgain_ref[...]).reshape(BATCH, TIME_TILE, D_MODEL)


def _ffn(h, gain, w1, w2, fgain):
    return pl.pallas_call(
        _ffn_kernel,
        grid=(N_TILES, D_FF // FF_TILE),
        in_specs=[
            pl.BlockSpec((None, ROWS, D_MODEL), lambda i, k: (i, 0, 0)),
            pl.BlockSpec((1, D_MODEL), lambda i, k: (0, 0)),
            pl.BlockSpec((D_MODEL, FF_TILE), lambda i, k: (0, k)),
            pl.BlockSpec((FF_TILE, D_MODEL), lambda i, k: (k, 0)),
            pl.BlockSpec((1, D_MODEL), lambda i, k: (0, 0)),
        ],
        out_specs=pl.BlockSpec((BATCH, TIME_TILE, D_MODEL), lambda i, k: (0, i, 0)),
        out_shape=jax.ShapeDtypeStruct((BATCH, SEQ, D_MODEL), F32),
        scratch_shapes=[pltpu.VMEM((ROWS, D_MODEL), BF16)],
        compiler_params=pltpu.CompilerParams(
            dimension_semantics=("arbitrary", "arbitrary"),
            vmem_limit_bytes=FFN_VMEM_LIMIT),
        name="ffn",
    )(h, gain, w1, w2, fgain)


def _ssm_params(a_re, a_im, log_dt, b_re, b_im, c_re, c_im):
    dt = jnp.exp(log_dt.astype(F32))[:, None]
    lam_re = jnp.minimum(a_re.astype(F32), -1e-4)
    lam_im = a_im.astype(F32)
    decay = jnp.exp(lam_re * dt)
    abar_re = decay * jnp.cos(lam_im * dt)
    abar_im = decay * jnp.sin(lam_im * dt)
    den = lam_re * lam_re + lam_im * lam_im
    num_re = abar_re - 1.0
    coef_re = (num_re * lam_re + abar_im * lam_im) / den
    coef_im = (abar_im * lam_re - num_re * lam_im) / den
    br = b_re.astype(F32)
    bi = b_im.astype(F32)
    bbar_re = coef_re[..., None] * br - coef_im[..., None] * bi
    bbar_im = coef_re[..., None] * bi + coef_im[..., None] * br

    g = GROUPS_PER_CHUNK

    def chunked(m):
        return jnp.swapaxes(m.reshape(N_CHUNKS, g, m.shape[1], m.shape[2]), 2, 3)

    b_parts = jnp.stack([chunked(bbar_re), chunked(bbar_im)], axis=3)
    b_rows = b_parts.reshape(N_CHUNKS, CHUNK_IN, 2, 1, SSM_STATE)
    b_tiled = jnp.broadcast_to(b_rows, (N_CHUNKS, CHUNK_IN, 2, g, SSM_STATE))
    row_group = lax.broadcasted_iota(jnp.int32, b_tiled.shape, 1) // SSM_GROUP
    col_group = lax.broadcasted_iota(jnp.int32, b_tiled.shape, 3)
    bd = jnp.where(row_group == col_group, b_tiled, 0.0)
    bd = bd.reshape(N_CHUNKS, CHUNK_IN, 2 * CHUNK_ST).astype(BF16)

    c_parts = jnp.stack([chunked(c_re.astype(F32)), chunked(c_im.astype(F32))], axis=1)
    c_rows = c_parts.reshape(N_CHUNKS, 2 * CHUNK_ST, 1, SSM_GROUP)
    c_tiled = jnp.broadcast_to(c_rows, (N_CHUNKS, 2 * CHUNK_ST, g, SSM_GROUP))
    row_group = (lax.broadcasted_iota(jnp.int32, c_tiled.shape, 1) % CHUNK_ST) // SSM_STATE
    col_group = lax.broadcasted_iota(jnp.int32, c_tiled.shape, 2)
    cd = jnp.where(row_group == col_group, c_tiled, 0.0)
    cd = cd.reshape(N_CHUNKS, 2 * CHUNK_ST, CHUNK_IN).astype(BF16)

    def per_chunk(a):
        a = a.reshape(N_CHUNKS, 1, CHUNK_ST)
        return jnp.broadcast_to(a, (N_CHUNKS, BATCH, CHUNK_ST))

    return bd, cd, per_chunk(abar_re), per_chunk(abar_im)


def kernel(x, w_in, ssm_a_re, ssm_a_im, ssm_log_dt, ssm_b_re, ssm_b_im, ssm_c_re, ssm_c_im,
           ssm_d, sgu_v_gain, sgu_w, sgu_b, out_gain_ssm, out_gain_sgu, w_out,
           mix_norm_gain, ffn_norm_gain, w_ff1, w_ff2, final_norm_gain):
    row = lambda v: v.astype(F32).reshape(1, -1)
    assert x.shape == (BATCH, SEQ, D_MODEL) and w_in.shape[0] == 1, "single-layer block only"
    layer = 0
    bd, cd, a_re, a_im = _ssm_params(
        ssm_a_re[layer], ssm_a_im[layer], ssm_log_dt[layer], ssm_b_re[layer],
        ssm_b_im[layer], ssm_c_re[layer], ssm_c_im[layer])
    bs = jnp.broadcast_to(sgu_b[layer].astype(F32)[:, :, None],
                          (SGU_HEADS, SGU_BLOCK, SGU_HEAD_DIM))
    proj, w_ff1_16, w_ff2_16, w_out_16 = _inproj(
        x, row(mix_norm_gain[layer]), w_in[layer].astype(BF16), w_ff1[layer], w_ff2[layer],
        w_out[layer])
    mixed = _mixer(proj, bd, cd, a_re, a_im, row(ssm_d[layer]),
                   row(sgu_v_gain[layer]), sgu_w[layer].astype(F32), bs,
                   row(out_gain_ssm[layer]), row(out_gain_sgu[layer]))
    h_tiles = _outproj(mixed, w_out_16, x)
    return _ffn(h_tiles, row(ffn_norm_gain[layer]), w_ff1_16, w_ff2_16, row(final_norm_gain))
```

```python
import math

import jax
import jax.numpy as jnp
from jax import lax
from jax.experimental import pallas as pl
from jax.experimental.pallas import tpu as pltpu

D_MODEL = 2048
BATCH = 8
SEQ = 2048
CHUNK = 64
D_SSM = 1024
D_SGU = 1024
SSM_GROUP = 16
SSM_GROUPS = 64
SSM_STATE = 64
SGU_BLOCK = 128
SGU_HEADS = 8
SGU_HEAD_DIM = 128
D_FF = 4 * D_MODEL
D_IN = 2 * D_SSM + 2 * D_SGU
EPS = 1e-5

LANES = 128
MXU_WIDTH = 256

TIME_TILE = SGU_BLOCK
ROWS = BATCH * TIME_TILE
N_TILES = SEQ // TIME_TILE
GROUPS_PER_CHUNK = 16
N_CHUNKS = SSM_GROUPS // GROUPS_PER_CHUNK
CHUNK_IN = GROUPS_PER_CHUNK * SSM_GROUP
CHUNK_ST = GROUPS_PER_CHUNK * SSM_STATE
GATING_PLAN = ((0,), (1, 2), (3, 4), (5, 6), (7,))
SIDE_STEPS_PER_PIECE = 4
READOUT_PARTS = 4
SCAN_UNROLL = 8
FF_TILE = 1024
FF_PIECE = 512
PROJ_TILE = 1024
N_PROJ_TILES = D_IN // PROJ_TILE
X_PART = D_MODEL // N_PROJ_TILES
N_CAST_STEPS = N_TILES * N_PROJ_TILES
VMEM_LIMIT = 56 * 1024 * 1024
FFN_VMEM_LIMIT = 60 * 1024 * 1024

F32 = jnp.float32
BF16 = jnp.bfloat16


def _rms(x, gain):
    return x * lax.rsqrt(jnp.mean(x * x, axis=-1, keepdims=True) + EPS) * gain


def _gelu(x):
    c = math.sqrt(2.0 / math.pi)
    half_x = 0.5 * x
    return half_x + half_x * jnp.tanh(x * (c + (0.044715 * c) * (x * x)))


def _sigmoid(x):
    return 1.0 / (1.0 + jnp.exp(-x))


def _lanes(j, base=0):
    return slice(base + j * LANES, base + (j + 1) * LANES)


def _inproj_kernel(x0_ref, x1_ref, x2_ref, x3_ref, gain_ref, w_ref, wff1_ref, wff2_ref, wout_ref,
                   o_ref, wff1_o, wff2_o, wout_o, hn_ref):
    j = pl.program_id(1)

    @pl.when(j == 0)
    def _():
        parts = [r[...].reshape(ROWS, X_PART) for r in (x0_ref, x1_ref, x2_ref, x3_ref)]
        ssq = sum(jnp.sum(p * p, axis=-1, keepdims=True) for p in parts)
        scale = lax.rsqrt(ssq * (1.0 / D_MODEL) + EPS)
        for q, p in enumerate(parts):
            cols = slice(q * X_PART, (q + 1) * X_PART)
            hn_ref[:, cols] = (p * scale * gain_ref[:, cols]).astype(BF16)

    w_cols = pl.ds(pl.multiple_of(j * PROJ_TILE, PROJ_TILE), PROJ_TILE)
    o_ref[...] = jnp.dot(hn_ref[...], w_ref[:, w_cols], preferred_element_type=F32).astype(BF16)
    wff1_o[...] = wff1_ref[...].astype(BF16)
    wff2_o[...] = wff2_ref[...].astype(BF16)
    wout_o[...] = wout_ref[...].astype(BF16)


def _inproj(x, gain, w_in_tiles, w_ff1, w_ff2, w_out):
    def x_part(q):
        return pl.BlockSpec(
            (BATCH, TIME_TILE, X_PART),
            lambda i, j: (0, jnp.minimum(i + jnp.where(j > q, 1, 0), N_TILES - 1), q))

    def cast_rows(rows, width):
        return pl.BlockSpec((rows // N_CAST_STEPS, width),
                            lambda i, j: (i * N_PROJ_TILES + j, 0))

    return pl.pallas_call(
        _inproj_kernel,
        grid=(N_TILES, N_PROJ_TILES),
        in_specs=[
            x_part(0), x_part(1), x_part(2), x_part(3),
            pl.BlockSpec((1, D_MODEL), lambda i, j: (0, 0)),
            pl.BlockSpec((D_MODEL, D_IN), lambda i, j: (0, 0), pipeline_mode=pl.Buffered(1)),
            cast_rows(D_MODEL, D_FF), cast_rows(D_FF, D_MODEL), cast_rows(D_MODEL, D_MODEL),
        ],
        out_specs=[
            pl.BlockSpec((None, ROWS, PROJ_TILE), lambda i, j: (i, 0, j)),
            cast_rows(D_MODEL, D_FF), cast_rows(D_FF, D_MODEL), cast_rows(D_MODEL, D_MODEL),
        ],
        out_shape=[
            jax.ShapeDtypeStruct((N_TILES, ROWS, D_IN), BF16),
            jax.ShapeDtypeStruct((D_MODEL, D_FF), BF16),
            jax.ShapeDtypeStruct((D_FF, D_MODEL), BF16),
            jax.ShapeDtypeStruct((D_MODEL, D_MODEL), BF16),
        ],
        scratch_shapes=[pltpu.VMEM((ROWS, D_MODEL), BF16)],
        compiler_params=pltpu.CompilerParams(
            dimension_semantics=("arbitrary", "arbitrary"),
            vmem_limit_bytes=VMEM_LIMIT),
        name="inproj",
    )(x, x, x, x, gain, w_in_tiles, w_ff1, w_ff2, w_out)


def _sgu_steps(b, ub_ref, vb_ref, vgain_ref, w_heads, bs_ref, gain_b_ref, o_ref):
    rows = pl.ds(b * TIME_TILE, TIME_TILE)
    v = _rms(_gelu(vb_ref[rows, :].astype(F32)), vgain_ref[...]).astype(BF16)
    yield
    pieces = []
    for h in range(SGU_HEADS):
        cols = slice(h * SGU_HEAD_DIM, (h + 1) * SGU_HEAD_DIM)
        mixed = jnp.dot(w_heads[h], v[:, cols], preferred_element_type=F32) + bs_ref[h]
        pieces.append(_gelu(ub_ref[rows, cols].astype(F32)) * mixed)
        yield
    yb = jnp.concatenate(pieces, axis=-1)
    o_ref[rows, D_SSM:D_SSM + D_SGU] = _rms(yb, gain_b_ref[...]).astype(BF16)
    yield


def _advance(gens, n):
    while n > 0 and gens:
        try:
            next(gens[0])
            n -= 1
        except StopIteration:
            gens.pop(0)


def _mixer_kernel(ua_ref, ga_ref, ub_ref, vb_ref, bd_ref, cd_ref, are_ref, aim_ref, dskip_ref,
                  vgain_ref, w_ref, bs_ref, gain_a_ref, gain_b_ref, o_ref,
                  state_ref, bu_ref, utb_ref, y_ref):
    @pl.when(pl.program_id(0) == 0)
    def _():
        state_ref[...] = jnp.zeros_like(state_ref)

    row_chunk = lax.broadcasted_iota(jnp.int32, (SGU_BLOCK, SGU_BLOCK), 0) // CHUNK
    col_chunk = lax.broadcasted_iota(jnp.int32, (SGU_BLOCK, SGU_BLOCK), 1) // CHUNK
    causal = col_chunk <= row_chunk
    w_heads = [jnp.where(causal, w_ref[h], 0.0).astype(BF16) for h in range(SGU_HEADS)]

    for b in range(BATCH):
        u_b = ua_ref[pl.ds(b * TIME_TILE, TIME_TILE), :].astype(F32)
        for s in range(D_SSM // LANES):
            utb_ref[s, pl.ds(b, TIME_TILE, stride=BATCH), :] = u_b[:, _lanes(s)]

    half = CHUNK_ST // LANES

    def gating(batches):
        return [_sgu_steps(b, ub_ref, vb_ref, vgain_ref, w_heads, bs_ref, gain_b_ref, o_ref)
                for b in batches]

    def input_matmul(kb, side):
        u_tb = jnp.concatenate([utb_ref[2 * kb], utb_ref[2 * kb + 1]], axis=1)
        u_tb16 = u_tb.astype(BF16)
        for jp in range(2 * CHUNK_ST // MXU_WIDTH):
            cols = slice(jp * MXU_WIDTH, (jp + 1) * MXU_WIDTH)
            bu_ref[:, cols] = jnp.dot(u_tb16, bd_ref[kb, :, cols], preferred_element_type=F32)
            _advance(side, SIDE_STEPS_PER_PIECE)
        return u_tb

    side = gating(GATING_PLAN[0])
    u_tb = input_matmul(0, side)
    _advance(side, 100)
    for kb in range(N_CHUNKS):
        a_re = [are_ref[kb, :, _lanes(j)] for j in range(half)]
        a_im = [aim_ref[kb, :, _lanes(j)] for j in range(half)]

        def step(t, carry):
            s_re, s_im = carry
            at_t = pl.ds(pl.multiple_of(t * BATCH, BATCH), BATCH)
            b_re = [bu_ref[at_t, _lanes(j)] for j in range(half)]
            b_im = [bu_ref[at_t, _lanes(j, CHUNK_ST)]
                    for j in range(half)]
            n_re = tuple(a_re[j] * s_re[j] - a_im[j] * s_im[j] + b_re[j] for j in range(half))
            n_im = tuple(a_re[j] * s_im[j] + a_im[j] * s_re[j] + b_im[j] for j in range(half))
            for j in range(half):
                bu_ref[at_t, _lanes(j)] = n_re[j]
                bu_ref[at_t, _lanes(j, CHUNK_ST)] = n_im[j]
            return n_re, n_im

        s0_re = tuple(state_ref[kb, :, _lanes(j)] for j in range(half))
        s0_im = tuple(state_ref[kb, :, _lanes(j, CHUNK_ST)]
                      for j in range(half))
        s_re, s_im = lax.fori_loop(0, TIME_TILE, step, (s0_re, s0_im), unroll=SCAN_UNROLL)
        for j in range(half):
            state_ref[kb, :, _lanes(j)] = s_re[j]
            state_ref[kb, :, _lanes(j, CHUNK_ST)] = s_im[j]

        side = gating(GATING_PLAN[kb + 1])
        cols = slice(kb * CHUNK_IN, (kb + 1) * CHUNK_IN)
        hk = CHUNK_ST // READOUT_PARTS

        def part(lo):
            return jnp.dot(bu_ref[:, lo:lo + hk].astype(BF16), cd_ref[kb, lo:lo + hk, :],
                           preferred_element_type=F32)

        y_tb = dskip_ref[:, cols] * u_tb
        for p in range(READOUT_PARTS):
            y_tb = y_tb + (part(p * hk) - part(CHUNK_ST + p * hk))
            _advance(side, 1)
        utb_ref[2 * kb] = y_tb[:, _lanes(0)]
        utb_ref[2 * kb + 1] = y_tb[:, _lanes(1)]

        def gate_steps(kb=kb, cols=cols):
            for b in range(BATCH):
                rows = pl.ds(b * TIME_TILE, TIME_TILE)
                y = jnp.concatenate(
                    [utb_ref[2 * kb + s, pl.ds(b, TIME_TILE, stride=BATCH), :]
                     for s in range(2)], axis=1)
                y_ref[rows, cols] = _gelu(y) * _sigmoid(ga_ref[rows, cols].astype(F32))
                yield

        side = [gate_steps()] + side
        if kb + 1 < N_CHUNKS:
            u_tb = input_matmul(kb + 1, side)
        _advance(side, 100)

    for b in range(BATCH):
        rows = pl.ds(b * TIME_TILE, TIME_TILE)
        o_ref[rows, 0:D_SSM] = _rms(y_ref[rows, :], gain_a_ref[...]).astype(BF16)


def _mixer(proj, bd, cd, a_re, a_im, d_skip, v_gain, w_s, bs, gain_a, gain_b):
    def const(shape):
        nd = len(shape)
        return pl.BlockSpec(shape, lambda i: (0,) * nd, pipeline_mode=pl.Buffered(1))

    def col(j):
        return pl.BlockSpec((None, ROWS, D_SSM), lambda i: (i, 0, j))

    return pl.pallas_call(
        _mixer_kernel,
        grid=(N_TILES,),
        in_specs=[
            col(0), col(1), col(2), col(3),
            const((N_CHUNKS, CHUNK_IN, 2 * CHUNK_ST)),
            const((N_CHUNKS, 2 * CHUNK_ST, CHUNK_IN)),
            const((N_CHUNKS, BATCH, CHUNK_ST)), const((N_CHUNKS, BATCH, CHUNK_ST)),
            const((1, D_SSM)), const((1, D_SGU)),
            const((SGU_HEADS, SGU_BLOCK, SGU_BLOCK)),
            const((SGU_HEADS, SGU_BLOCK, SGU_HEAD_DIM)),
            const((1, D_SSM)), const((1, D_SGU)),
        ],
        out_specs=pl.BlockSpec((None, ROWS, D_MODEL), lambda i: (i, 0, 0)),
        out_shape=jax.ShapeDtypeStruct((N_TILES, ROWS, D_MODEL), BF16),
        scratch_shapes=[
            pltpu.VMEM((N_CHUNKS, BATCH, 2 * CHUNK_ST), F32),
            pltpu.VMEM((ROWS, 2 * CHUNK_ST), F32),
            pltpu.VMEM((D_SSM // LANES, ROWS, LANES), F32),
            pltpu.VMEM((ROWS, D_SSM), F32),
        ],
        compiler_params=pltpu.CompilerParams(
            dimension_semantics=("arbitrary",),
            vmem_limit_bytes=VMEM_LIMIT),
        name="mixer",
    )(proj, proj, proj, proj, bd, cd, a_re, a_im, d_skip, v_gain, w_s, bs, gain_a, gain_b)


def _outproj_kernel(m_ref, w_ref, x_ref, o_ref):
    for c in range(D_MODEL // PROJ_TILE):
        cols = slice(c * PROJ_TILE, (c + 1) * PROJ_TILE)
        acc = jnp.dot(m_ref[...], w_ref[:, cols], preferred_element_type=F32)
        o_ref[:, cols] = x_ref[:, :, cols].reshape(ROWS, PROJ_TILE) + acc


def _outproj(mixed, w_out, x):
    return pl.pallas_call(
        _outproj_kernel,
        grid=(N_TILES,),
        in_specs=[
            pl.BlockSpec((None, ROWS, D_MODEL), lambda i: (i, 0, 0)),
            pl.BlockSpec((D_MODEL, D_MODEL), lambda i: (0, 0), pipeline_mode=pl.Buffered(1)),
            pl.BlockSpec((BATCH, TIME_TILE, D_MODEL), lambda i: (0, i, 0)),
        ],
        out_specs=pl.BlockSpec((None, ROWS, D_MODEL), lambda i: (i, 0, 0)),
        out_shape=jax.ShapeDtypeStruct((N_TILES, ROWS, D_MODEL), F32),
        compiler_params=pltpu.CompilerParams(
            dimension_semantics=("arbitrary",),
            vmem_limit_bytes=VMEM_LIMIT),
        name="outproj",
    )(mixed, w_out, x)


def _ffn_kernel(h_ref, gain_ref, w1_ref, w2_ref, fgain_ref, o_ref, hn_ref):
    k = pl.program_id(1)

    @pl.when(k == 0)
    def _():
        h = h_ref[...]
        hn_ref[...] = _rms(h, gain_ref[...]).astype(BF16)
        o_ref[...] = h.reshape(BATCH, TIME_TILE, D_MODEL)

    acts = []
    for p in range(FF_TILE // FF_PIECE):
        z = jnp.dot(hn_ref[...], w1_ref[:, p * FF_PIECE:(p + 1) * FF_PIECE],
                    preferred_element_type=F32)
        z = jnp.maximum(z, 0.0)
        acts.append((z * z).astype(BF16))
    a = jnp.concatenate(acts, axis=1)
    for c in range(D_MODEL // FF_PIECE):
        cols = slice(c * FF_PIECE, (c + 1) * FF_PIECE)
        upd = jnp.dot(a, w2_ref[:, cols], preferred_element_type=F32)
        o_ref[:, :, cols] += upd.reshape(BATCH, TIME_TILE, FF_PIECE)

    @pl.when(k == pl.num_programs(1) - 1)
    def _():
        acc = o_ref[...].reshape(ROWS, D_MODEL)
        o_ref[...] = _rms(acc, fgain_ref[...]).reshape(BATCH, TIME_TILE, D_MODEL)


def _ffn(h, gain, w1, w2, fgain):
    return pl.pallas_call(
        _ffn_kernel,
        grid=(N_TILES, D_FF // FF_TILE),
        in_specs=[
            pl.BlockSpec((None, ROWS, D_MODEL), lambda i, k: (i, 0, 0)),
            pl.BlockSpec((1, D_MODEL), lambda i, k: (0, 0)),
            pl.BlockSpec((D_MODEL, FF_TILE), lambda i, k: (0, k)),
            pl.BlockSpec((FF_TILE, D_MODEL), lambda i, k: (k, 0)),
            pl.BlockSpec((1, D_MODEL), lambda i, k: (0, 0)),
        ],
        out_specs=pl.BlockSpec((BATCH, TIME_TILE, D_MODEL), lambda i, k: (0, i, 0)),
        out_shape=jax.ShapeDtypeStruct((BATCH, SEQ, D_MODEL), F32),
        scratch_shapes=[pltpu.VMEM((ROWS, D_MODEL), BF16)],
        compiler_params=pltpu.CompilerParams(
            dimension_semantics=("arbitrary", "arbitrary"),
            vmem_limit_bytes=FFN_VMEM_LIMIT),
        name="ffn",
    )(h, gain, w1, w2, fgain)


def _ssm_params(a_re, a_im, log_dt, b_re, b_im, c_re, c_im):
    dt = jnp.exp(log_dt.astype(F32))[:, None]
    lam_re = jnp.minimum(a_re.astype(F32), -1e-4)
    lam_im = a_im.astype(F32)
    decay = jnp.exp(lam_re * dt)
    abar_re = decay * jnp.cos(lam_im * dt)
    abar_im = decay * jnp.sin(lam_im * dt)
    den = lam_re * lam_re + lam_im * lam_im
    num_re = abar_re - 1.0
    coef_re = (num_re * lam_re + abar_im * lam_im) / den
    coef_im = (abar_im * lam_re - num_re * lam_im) / den
    br = b_re.astype(F32)
    bi = b_im.astype(F32)
    bbar_re = coef_re[..., None] * br - coef_im[..., None] * bi
    bbar_im = coef_re[..., None] * bi + coef_im[..., None] * br

    g = GROUPS_PER_CHUNK

    def chunked(m):
        return jnp.swapaxes(m.reshape(N_CHUNKS, g, m.shape[1], m.shape[2]), 2, 3)

    b_parts = jnp.stack([chunked(bbar_re), chunked(bbar_im)], axis=3)
    b_rows = b_parts.reshape(N_CHUNKS, CHUNK_IN, 2, 1, SSM_STATE)
    b_tiled = jnp.broadcast_to(b_rows, (N_CHUNKS, CHUNK_IN, 2, g, SSM_STATE))
    row_group = lax.broadcasted_iota(jnp.int32, b_tiled.shape, 1) // SSM_GROUP
    col_group = lax.broadcasted_iota(jnp.int32, b_tiled.shape, 3)
    bd = jnp.where(row_group == col_group, b_tiled, 0.0)
    bd = bd.reshape(N_CHUNKS, CHUNK_IN, 2 * CHUNK_ST).astype(BF16)

    c_parts = jnp.stack([chunked(c_re.astype(F32)), chunked(c_im.astype(F32))], axis=1)
    c_rows = c_parts.reshape(N_CHUNKS, 2 * CHUNK_ST, 1, SSM_GROUP)
    c_tiled = jnp.broadcast_to(c_rows, (N_CHUNKS, 2 * CHUNK_ST, g, SSM_GROUP))
    row_group = (lax.broadcasted_iota(jnp.int32, c_tiled.shape, 1) % CHUNK_ST) // SSM_STATE
    col_group = lax.broadcasted_iota(jnp.int32, c_tiled.shape, 2)
    cd = jnp.where(row_group == col_group, c_tiled, 0.0)
    cd = cd.reshape(N_CHUNKS, 2 * CHUNK_ST, CHUNK_IN).astype(BF16)

    def per_chunk(a):
        a = a.reshape(N_CHUNKS, 1, CHUNK_ST)
        return jnp.broadcast_to(a, (N_CHUNKS, BATCH, CHUNK_ST))

    return bd, cd, per_chunk(abar_re), per_chunk(abar_im)


def kernel(x, w_in, ssm_a_re, ssm_a_im, ssm_log_dt, ssm_b_re, ssm_b_im, ssm_c_re, ssm_c_im,
           ssm_d, sgu_v_gain, sgu_w, sgu_b, out_gain_ssm, out_gain_sgu, w_out,
           mix_norm_gain, ffn_norm_gain, w_ff1, w_ff2, final_norm_gain):
    row = lambda v: v.astype(F32).reshape(1, -1)
    assert x.shape == (BATCH, SEQ, D_MODEL) and w_in.shape[0] == 1, "single-layer block only"
    layer = 0
    bd, cd, a_re, a_im = _ssm_params(
        ssm_a_re[layer], ssm_a_im[layer], ssm_log_dt[layer], ssm_b_re[layer],
        ssm_b_im[layer], ssm_c_re[layer], ssm_c_im[layer])
    bs = jnp.broadcast_to(sgu_b[layer].astype(F32)[:, :, None],
                          (SGU_HEADS, SGU_BLOCK, SGU_HEAD_DIM))
    proj, w_ff1_16, w_ff2_16, w_out_16 = _inproj(
        x, row(mix_norm_gain[layer]), w_in[layer].astype(BF16), w_ff1[layer], w_ff2[layer],
        w_out[layer])
    mixed = _mixer(proj, bd, cd, a_re, a_im, row(ssm_d[layer]),
                   row(sgu_v_gain[layer]), sgu_w[layer].astype(F32), bs,
                   row(out_gain_ssm[layer]), row(out_gain_sgu[layer]))
    h_tiles = _outproj(mixed, w_out_16, x)
    return _ffn(h_tiles, row(ffn_norm_gain[layer]), w_ff1_16, w_ff2_16, row(final_norm_gain))
```

```python
import math

import jax
import jax.numpy as jnp
from jax import lax
from jax.experimental import pallas as pl
from jax.experimental.pallas import tpu as pltpu

D_MODEL = 2048
BATCH = 8
SEQ = 2048
CHUNK = 64
D_SSM = 1024
D_SGU = 1024
SSM_GROUP = 16
SSM_GROUPS = 64
SSM_STATE = 64
SGU_BLOCK = 128
SGU_HEADS = 8
SGU_HEAD_DIM = 128
D_FF = 4 * D_MODEL
D_IN = 2 * D_SSM + 2 * D_SGU
EPS = 1e-5

LANES = 128
MXU_WIDTH = 256

TIME_TILE = SGU_BLOCK
ROWS = BATCH * TIME_TILE
N_TILES = SEQ // TIME_TILE
GROUPS_PER_CHUNK = 8
N_CHUNKS = SSM_GROUPS // GROUPS_PER_CHUNK
CHUNK_IN = GROUPS_PER_CHUNK * SSM_GROUP
CHUNK_ST = GROUPS_PER_CHUNK * SSM_STATE
SCAN_GROUP = 8
MXU_PIECES_PER_GROUP = 1
SIDE_STEPS_PER_GROUP = 2
READOUT_PARTS = 2
FF_TILE = 1024
FF_PIECE = 512
PROJ_TILE = 1024
N_PROJ_TILES = D_IN // PROJ_TILE
X_PART = D_MODEL // N_PROJ_TILES
N_CAST_STEPS = N_TILES * N_PROJ_TILES
VMEM_LIMIT = 56 * 1024 * 1024
FFN_VMEM_LIMIT = 60 * 1024 * 1024

F32 = jnp.float32
BF16 = jnp.bfloat16


def _rms(x, gain):
    return x * lax.rsqrt(jnp.mean(x * x, axis=-1, keepdims=True) + EPS) * gain


def _gelu(x):
    c = math.sqrt(2.0 / math.pi)
    half_x = 0.5 * x
    return half_x + half_x * jnp.tanh(x * (c + (0.044715 * c) * (x * x)))


def _sigmoid(x):
    return 1.0 / (1.0 + jnp.exp(-x))


def _lanes(j, base=0):
    return slice(base + j * LANES, base + (j + 1) * LANES)


def _inproj_kernel(x0_ref, x1_ref, x2_ref, x3_ref, gain_ref, w_ref, wff1_ref, wff2_ref, wout_ref,
                   o_ref, wff1_o, wff2_o, wout_o, hn_ref):
    j = pl.program_id(1)

    @pl.when(j == 0)
    def _():
        parts = [r[...].reshape(ROWS, X_PART) for r in (x0_ref, x1_ref, x2_ref, x3_ref)]
        ssq = sum(jnp.sum(p * p, axis=-1, keepdims=True) for p in parts)
        scale = lax.rsqrt(ssq * (1.0 / D_MODEL) + EPS)
        for q, p in enumerate(parts):
            cols = slice(q * X_PART, (q + 1) * X_PART)
            hn_ref[:, cols] = (p * scale * gain_ref[:, cols]).astype(BF16)

    w_cols = pl.ds(pl.multiple_of(j * PROJ_TILE, PROJ_TILE), PROJ_TILE)
    o_ref[...] = jnp.dot(hn_ref[...], w_ref[:, w_cols], preferred_element_type=F32).astype(BF16)
    wff1_o[...] = wff1_ref[...].astype(BF16)
    wff2_o[...] = wff2_ref[...].astype(BF16)
    wout_o[...] = wout_ref[...].astype(BF16)


def _inproj(x, gain, w_in_tiles, w_ff1, w_ff2, w_out):
    def x_part(q):
        return pl.BlockSpec(
            (BATCH, TIME_TILE, X_PART),
            lambda i, j: (0, jnp.minimum(i + jnp.where(j > q, 1, 0), N_TILES - 1), q))

    def cast_rows(rows, width):
        return pl.BlockSpec((rows // N_CAST_STEPS, width),
                            lambda i, j: (i * N_PROJ_TILES + j, 0))

    return pl.pallas_call(
        _inproj_kernel,
        grid=(N_TILES, N_PROJ_TILES),
        in_specs=[
            x_part(0), x_part(1), x_part(2), x_part(3),
            pl.BlockSpec((1, D_MODEL), lambda i, j: (0, 0)),
            pl.BlockSpec((D_MODEL, D_IN), lambda i, j: (0, 0), pipeline_mode=pl.Buffered(1)),
            cast_rows(D_MODEL, D_FF), cast_rows(D_FF, D_MODEL), cast_rows(D_MODEL, D_MODEL),
        ],
        out_specs=[
            pl.BlockSpec((None, ROWS, PROJ_TILE), lambda i, j: (i, 0, j)),
            cast_rows(D_MODEL, D_FF), cast_rows(D_FF, D_MODEL), cast_rows(D_MODEL, D_MODEL),
        ],
        out_shape=[
            jax.ShapeDtypeStruct((N_TILES, ROWS, D_IN), BF16),
            jax.ShapeDtypeStruct((D_MODEL, D_FF), BF16),
            jax.ShapeDtypeStruct((D_FF, D_MODEL), BF16),
            jax.ShapeDtypeStruct((D_MODEL, D_MODEL), BF16),
        ],
        scratch_shapes=[pltpu.VMEM((ROWS, D_MODEL), BF16)],
        compiler_params=pltpu.CompilerParams(
            dimension_semantics=("arbitrary", "arbitrary"),
            vmem_limit_bytes=VMEM_LIMIT),
        name="inproj",
    )(x, x, x, x, gain, w_in_tiles, w_ff1, w_ff2, w_out)


def _sgu_steps(b, ub_ref, vb_ref, vgain_ref, w_heads, bs_ref, gain_b_ref, o_ref):
    rows = pl.ds(b * TIME_TILE, TIME_TILE)
    v = _rms(_gelu(vb_ref[rows, :].astype(F32)), vgain_ref[...]).astype(BF16)
    yield
    pieces = []
    for h in range(SGU_HEADS):
        cols = slice(h * SGU_HEAD_DIM, (h + 1) * SGU_HEAD_DIM)
        mixed = jnp.dot(w_heads[h], v[:, cols], preferred_element_type=F32) + bs_ref[h]
        pieces.append(_gelu(ub_ref[rows, cols].astype(F32)) * mixed)
        yield
    yb = jnp.concatenate(pieces, axis=-1)
    o_ref[rows, D_SSM:D_SSM + D_SGU] = _rms(yb, gain_b_ref[...]).astype(BF16)
    yield


def _drive(stream, n):
    for _ in range(n):
        try:
            next(stream)
        except StopIteration:
            return False
    return True


def _mixer_kernel(ua_ref, ga_ref, ub_ref, vb_ref, bd_ref, cd_ref, are_ref, aim_ref, dskip_ref,
                  vgain_ref, w_ref, bs_ref, gain_a_ref, gain_b_ref, o_ref,
                  state_ref, bu_ref, utb_ref, y_ref):
    @pl.when(pl.program_id(0) == 0)
    def _():
        state_ref[...] = jnp.zeros_like(state_ref)

    row_chunk = lax.broadcasted_iota(jnp.int32, (SGU_BLOCK, SGU_BLOCK), 0) // CHUNK
    col_chunk = lax.broadcasted_iota(jnp.int32, (SGU_BLOCK, SGU_BLOCK), 1) // CHUNK
    causal = col_chunk <= row_chunk
    w_heads = [jnp.where(causal, w_ref[h], 0.0).astype(BF16) for h in range(SGU_HEADS)]

    for b in range(BATCH):
        u_b = ua_ref[pl.ds(b * TIME_TILE, TIME_TILE), :].astype(F32)
        for s in range(N_CHUNKS):
            utb_ref[s, pl.ds(b, TIME_TILE, stride=BATCH), :] = u_b[:, _lanes(s)]

    half = CHUNK_ST // LANES
    readout_done = [False] * N_CHUNKS

    def input_matmul(kb):
        u16 = utb_ref[kb].astype(BF16)
        for jp in range(2 * CHUNK_ST // MXU_WIDTH):
            cols = slice(jp * MXU_WIDTH, (jp + 1) * MXU_WIDTH)
            bu_ref[kb % 2, :, cols] = jnp.dot(u16, bd_ref[kb, :, cols],
                                             preferred_element_type=F32)
            yield

    def readout(kb):
        hk = CHUNK_ST // READOUT_PARTS

        def part(lo):
            return jnp.dot(bu_ref[kb % 2, :, lo:lo + hk].astype(BF16), cd_ref[kb, lo:lo + hk, :],
                           preferred_element_type=F32)

        y_tb = dskip_ref[:, _lanes(kb)] * utb_ref[kb]
        for p in range(READOUT_PARTS):
            y_tb = y_tb + (part(p * hk) - part(CHUNK_ST + p * hk))
            yield
        utb_ref[kb] = y_tb
        readout_done[kb] = True
        yield

    def gate_steps(kb):
        for b in range(BATCH):
            rows = pl.ds(b * TIME_TILE, TIME_TILE)
            y = utb_ref[kb, pl.ds(b, TIME_TILE, stride=BATCH), :]
            y_ref[rows, _lanes(kb)] = _gelu(y) * _sigmoid(ga_ref[rows, _lanes(kb)].astype(F32))
            yield

    def scan(kb):
        a_re = [are_ref[kb, :, _lanes(j)] for j in range(half)]
        a_im = [aim_ref[kb, :, _lanes(j)] for j in range(half)]
        s_re = [state_ref[kb, :, _lanes(j)] for j in range(half)]
        s_im = [state_ref[kb, :, _lanes(j, CHUNK_ST)] for j in range(half)]
        buf = kb % 2
        for t in range(TIME_TILE):
            rows = slice(t * BATCH, (t + 1) * BATCH)
            b_re = [bu_ref[buf, rows, _lanes(j)] for j in range(half)]
            b_im = [bu_ref[buf, rows, _lanes(j, CHUNK_ST)] for j in range(half)]
            n_re = [a_re[j] * s_re[j] - a_im[j] * s_im[j] + b_re[j] for j in range(half)]
            n_im = [a_re[j] * s_im[j] + a_im[j] * s_re[j] + b_im[j] for j in range(half)]
            for j in range(half):
                bu_ref[buf, rows, _lanes(j)] = n_re[j]
                bu_ref[buf, rows, _lanes(j, CHUNK_ST)] = n_im[j]
            s_re, s_im = n_re, n_im
            if (t + 1) % SCAN_GROUP == 0:
                yield
        for j in range(half):
            state_ref[kb, :, _lanes(j)] = s_re[j]
            state_ref[kb, :, _lanes(j, CHUNK_ST)] = s_im[j]

    def chain(*gens):
        for g in gens:
            yield from g

    for _ in input_matmul(0):
        pass
    for kb in range(N_CHUNKS):
        mxu = chain(*([readout(kb - 1)] if kb > 0 else []),
                    *([input_matmul(kb + 1)] if kb + 1 < N_CHUNKS else []))
        side = chain(_sgu_steps(kb, ub_ref, vb_ref, vgain_ref, w_heads, bs_ref, gain_b_ref, o_ref),
                     *([gate_steps(kb - 1)] if kb > 0 else []))
        sgu_left = SGU_HEADS + 2
        for _ in scan(kb):
            _drive(mxu, MXU_PIECES_PER_GROUP)
            for _ in range(SIDE_STEPS_PER_GROUP):
                if sgu_left == 0 and kb > 0 and not readout_done[kb - 1]:
                    break
                _drive(side, 1)
                sgu_left = max(sgu_left - 1, 0)
        _drive(mxu, 1000)
        _drive(side, 1000)
    last = N_CHUNKS - 1
    _drive(readout(last), 1000)
    _drive(gate_steps(last), 1000)

    for b in range(BATCH):
        rows = pl.ds(b * TIME_TILE, TIME_TILE)
        o_ref[rows, 0:D_SSM] = _rms(y_ref[rows, :], gain_a_ref[...]).astype(BF16)


def _mixer(proj, bd, cd, a_re, a_im, d_skip, v_gain, w_s, bs, gain_a, gain_b):
    def const(shape):
        nd = len(shape)
        return pl.BlockSpec(shape, lambda i: (0,) * nd, pipeline_mode=pl.Buffered(1))

    def col(j):
        return pl.BlockSpec((None, ROWS, D_SSM), lambda i: (i, 0, j))

    return pl.pallas_call(
        _mixer_kernel,
        grid=(N_TILES,),
        in_specs=[
            col(0), col(1), col(2), col(3),
            const((N_CHUNKS, CHUNK_IN, 2 * CHUNK_ST)),
            const((N_CHUNKS, 2 * CHUNK_ST, CHUNK_IN)),
            const((N_CHUNKS, BATCH, CHUNK_ST)), const((N_CHUNKS, BATCH, CHUNK_ST)),
            const((1, D_SSM)), const((1, D_SGU)),
            const((SGU_HEADS, SGU_BLOCK, SGU_BLOCK)),
            const((SGU_HEADS, SGU_BLOCK, SGU_HEAD_DIM)),
            const((1, D_SSM)), const((1, D_SGU)),
        ],
        out_specs=pl.BlockSpec((None, ROWS, D_MODEL), lambda i: (i, 0, 0)),
        out_shape=jax.ShapeDtypeStruct((N_TILES, ROWS, D_MODEL), BF16),
        scratch_shapes=[
            pltpu.VMEM((N_CHUNKS, BATCH, 2 * CHUNK_ST), F32),
            pltpu.VMEM((2, ROWS, 2 * CHUNK_ST), F32),
            pltpu.VMEM((D_SSM // LANES, ROWS, LANES), F32),
            pltpu.VMEM((ROWS, D_SSM), F32),
        ],
        compiler_params=pltpu.CompilerParams(
            dimension_semantics=("arbitrary",),
            vmem_limit_bytes=VMEM_LIMIT),
        name="mixer",
    )(proj, proj, proj, proj, bd, cd, a_re, a_im, d_skip, v_gain, w_s, bs, gain_a, gain_b)


def _outproj_kernel(m_ref, w_ref, x_ref, o_ref):
    for c in range(D_MODEL // PROJ_TILE):
        cols = slice(c * PROJ_TILE, (c + 1) * PROJ_TILE)
        acc = jnp.dot(m_ref[...], w_ref[:, cols], preferred_element_type=F32)
        o_ref[:, cols] = x_ref[:, :, cols].reshape(ROWS, PROJ_TILE) + acc


def _outproj(mixed, w_out, x):
    return pl.pallas_call(
        _outproj_kernel,
        grid=(N_TILES,),
        in_specs=[
            pl.BlockSpec((None, ROWS, D_MODEL), lambda i: (i, 0, 0)),
            pl.BlockSpec((D_MODEL, D_MODEL), lambda i: (0, 0), pipeline_mode=pl.Buffered(1)),
            pl.BlockSpec((BATCH, TIME_TILE, D_MODEL), lambda i: (0, i, 0)),
        ],
        out_specs=pl.BlockSpec((None, ROWS, D_MODEL), lambda i: (i, 0, 0)),
        out_shape=jax.ShapeDtypeStruct((N_TILES, ROWS, D_MODEL), F32),
        compiler_params=pltpu.CompilerParams(
            dimension_semantics=("arbitrary",),
            vmem_limit_bytes=VMEM_LIMIT),
        name="outproj",
    )(mixed, w_out, x)


def _ffn_kernel(h_ref, gain_ref, w1_ref, w2_ref, fgain_ref, o_ref, hn_ref):
    k = pl.program_id(1)

    @pl.when(k == 0)
    def _():
        h = h_ref[...]
        hn_ref[...] = _rms(h, gain_ref[...]).astype(BF16)
        o_ref[...] = h.reshape(BATCH, TIME_TILE, D_MODEL)

    acts = []
    for p in range(FF_TILE // FF_PIECE):
        z = jnp.dot(hn_ref[...], w1_ref[:, p * FF_PIECE:(p + 1) * FF_PIECE],
                    preferred_element_type=F32)
        z = jnp.maximum(z, 0.0)
        acts.append((z * z).astype(BF16))
    a = jnp.concatenate(acts, axis=1)
    for c in range(D_MODEL // FF_PIECE):
        cols = slice(c * FF_PIECE, (c + 1) * FF_PIECE)
        upd = jnp.dot(a, w2_ref[:, cols], preferred_element_type=F32)
        o_ref[:, :, cols] += upd.reshape(BATCH, TIME_TILE, FF_PIECE)

    @pl.when(k == pl.num_programs(1) - 1)
    def _():
        acc = o_ref[...].reshape(ROWS, D_MODEL)
        o_ref[...] = _rms(acc, fgain_ref[...]).reshape(BATCH, TIME_TILE, D_MODEL)


def _ffn(h, gain, w1, w2, fgain):
    return pl.pallas_call(
        _ffn_kernel,
        grid=(N_TILES, D_FF // FF_TILE),
        in_specs=[
            pl.BlockSpec((None, ROWS, D_MODEL), lambda i, k: (i, 0, 0)),
            pl.BlockSpec((1, D_MODEL), lambda i, k: (0, 0)),
            pl.BlockSpec((D_MODEL, FF_TILE), lambda i, k: (0, k)),
            pl.BlockSpec((FF_TILE, D_MODEL), lambda i, k: (k, 0)),
            pl.BlockSpec((1, D_MODEL), lambda i, k: (0, 0)),
        ],
        out_specs=pl.BlockSpec((BATCH, TIME_TILE, D_MODEL), lambda i, k: (0, i, 0)),
        out_shape=jax.ShapeDtypeStruct((BATCH, SEQ, D_MODEL), F32),
        scratch_shapes=[pltpu.VMEM((ROWS, D_MODEL), BF16)],
        compiler_params=pltpu.CompilerParams(
            dimension_semantics=("arbitrary", "arbitrary"),
            vmem_limit_bytes=FFN_VMEM_LIMIT),
        name="ffn",
    )(h, gain, w1, w2, fgain)


def _ssm_params(a_re, a_im, log_dt, b_re, b_im, c_re, c_im):
    dt = jnp.exp(log_dt.astype(F32))[:, None]
    lam_re = jnp.minimum(a_re.astype(F32), -1e-4)
    lam_im = a_im.astype(F32)
    decay = jnp.exp(lam_re * dt)
    abar_re = decay * jnp.cos(lam_im * dt)
    abar_im = decay * jnp.sin(lam_im * dt)
    den = lam_re * lam_re + lam_im * lam_im
    num_re = abar_re - 1.0
    coef_re = (num_re * lam_re + abar_im * lam_im) / den
    coef_im = (abar_im * lam_re - num_re * lam_im) / den
    br = b_re.astype(F32)
    bi = b_im.astype(F32)
    bbar_re = coef_re[..., None] * br - coef_im[..., None] * bi
    bbar_im = coef_re[..., None] * bi + coef_im[..., None] * br

    g = GROUPS_PER_CHUNK

    def chunked(m):
        return jnp.swapaxes(m.reshape(N_CHUNKS, g, m.shape[1], m.shape[2]), 2, 3)

    b_parts = jnp.stack([chunked(bbar_re), chunked(bbar_im)], axis=3)
    b_rows = b_parts.reshape(N_CHUNKS, CHUNK_IN, 2, 1, SSM_STATE)
    b_tiled = jnp.broadcast_to(b_rows, (N_CHUNKS, CHUNK_IN, 2, g, SSM_STATE))
    row_group = lax.broadcasted_iota(jnp.int32, b_tiled.shape, 1) // SSM_GROUP
    col_group = lax.broadcasted_iota(jnp.int32, b_tiled.shape, 3)
    bd = jnp.where(row_group == col_group, b_tiled, 0.0)
    bd = bd.reshape(N_CHUNKS, CHUNK_IN, 2 * CHUNK_ST).astype(BF16)

    c_parts = jnp.stack([chunked(c_re.astype(F32)), chunked(c_im.astype(F32))], axis=1)
    c_rows = c_parts.reshape(N_CHUNKS, 2 * CHUNK_ST, 1, SSM_GROUP)
    c_tiled = jnp.broadcast_to(c_rows, (N_CHUNKS, 2 * CHUNK_ST, g, SSM_GROUP))
    row_group = (lax.broadcasted_iota(jnp.int32, c_tiled.shape, 1) % CHUNK_ST) // SSM_STATE
    col_group = lax.broadcasted_iota(jnp.int32, c_tiled.shape, 2)
    cd = jnp.where(row_group == col_group, c_tiled, 0.0)
    cd = cd.reshape(N_CHUNKS, 2 * CHUNK_ST, CHUNK_IN).astype(BF16)

    def per_chunk(a):
        a = a.reshape(N_CHUNKS, 1, CHUNK_ST)
        return jnp.broadcast_to(a, (N_CHUNKS, BATCH, CHUNK_ST))

    return bd, cd, per_chunk(abar_re), per_chunk(abar_im)


def kernel(x, w_in, ssm_a_re, ssm_a_im, ssm_log_dt, ssm_b_re, ssm_b_im, ssm_c_re, ssm_c_im,
           ssm_d, sgu_v_gain, sgu_w, sgu_b, out_gain_ssm, out_gain_sgu, w_out,
           mix_norm_gain, ffn_norm_gain, w_ff1, w_ff2, final_norm_gain):
    row = lambda v: v.astype(F32).reshape(1, -1)
    assert x.shape == (BATCH, SEQ, D_MODEL) and w_in.shape[0] == 1, "single-layer block only"
    layer = 0
    bd, cd, a_re, a_im = _ssm_params(
        ssm_a_re[layer], ssm_a_im[layer], ssm_log_dt[layer], ssm_b_re[layer],
        ssm_b_im[layer], ssm_c_re[layer], ssm_c_im[layer])
    bs = jnp.broadcast_to(sgu_b[layer].astype(F32)[:, :, None],
                          (SGU_HEADS, SGU_BLOCK, SGU_HEAD_DIM))
    proj, w_ff1_16, w_ff2_16, w_out_16 = _inproj(
        x, row(mix_norm_gain[layer]), w_in[layer].astype(BF16), w_ff1[layer], w_ff2[layer],
        w_out[layer])
    mixed = _mixer(proj, bd, cd, a_re, a_im, row(ssm_d[layer]),
                   row(sgu_v_gain[layer]), sgu_w[layer].astype(F32), bs,
                   row(out_gain_ssm[layer]), row(out_gain_sgu[layer]))
    h_tiles = _outproj(mixed, w_out_16, x)
    return _ffn(h_tiles, row(ffn_norm_gain[layer]), w_ff1_16, w_ff2_16, row(final_norm_gain))
```

```python
import math

import jax
import jax.numpy as jnp
from jax import lax
from jax.experimental import pallas as pl
from jax.experimental.pallas import tpu as pltpu

D_MODEL = 2048
BATCH = 8
SEQ = 2048
CHUNK = 64
D_SSM = 1024
D_SGU = 1024
SSM_GROUP = 16
SSM_GROUPS = 64
SSM_STATE = 64
SGU_BLOCK = 128
SGU_HEADS = 8
SGU_HEAD_DIM = 128
D_FF = 4 * D_MODEL
D_IN = 2 * D_SSM + 2 * D_SGU
EPS = 1e-5

LANES = 128
MXU_WIDTH = 256

TIME_TILE = SGU_BLOCK
ROWS = BATCH * TIME_TILE
N_TILES = SEQ // TIME_TILE
GROUPS_PER_CHUNK = 8
N_CHUNKS = SSM_GROUPS // GROUPS_PER_CHUNK
CHUNK_IN = GROUPS_PER_CHUNK * SSM_GROUP
CHUNK_ST = GROUPS_PER_CHUNK * SSM_STATE
SCAN_GROUP = 4
MXU_PIECES_PER_GROUP = 1
SIDE_STEPS_PER_GROUP = 1
READOUT_PARTS = 2
FF_TILE = 1024
FF_PIECE = 512
PROJ_TILE = 1024
N_PROJ_TILES = D_IN // PROJ_TILE
X_PART = D_MODEL // N_PROJ_TILES
N_CAST_STEPS = N_TILES * N_PROJ_TILES
VMEM_LIMIT = 56 * 1024 * 1024
FFN_VMEM_LIMIT = 60 * 1024 * 1024

F32 = jnp.float32
BF16 = jnp.bfloat16


def _rms(x, gain):
    return x * lax.rsqrt(jnp.mean(x * x, axis=-1, keepdims=True) + EPS) * gain


def _gelu(x):
    c = math.sqrt(2.0 / math.pi)
    half_x = 0.5 * x
    return half_x + half_x * jnp.tanh(x * (c + (0.044715 * c) * (x * x)))


def _sigmoid(x):
    return 1.0 / (1.0 + jnp.exp(-x))


def _lanes(j, base=0):
    return slice(base + j * LANES, base + (j + 1) * LANES)


def _inproj_kernel(x0_ref, x1_ref, x2_ref, x3_ref, gain_ref, w_ref, wff1_ref, wff2_ref, wout_ref,
                   o_ref, wff1_o, wff2_o, wout_o, hn_ref):
    j = pl.program_id(1)

    @pl.when(j == 0)
    def _():
        parts = [r[...].reshape(ROWS, X_PART) for r in (x0_ref, x1_ref, x2_ref, x3_ref)]
        ssq = sum(jnp.sum(p * p, axis=-1, keepdims=True) for p in parts)
        scale = lax.rsqrt(ssq * (1.0 / D_MODEL) + EPS)
        for q, p in enumerate(parts):
            cols = slice(q * X_PART, (q + 1) * X_PART)
            hn_ref[:, cols] = (p * scale * gain_ref[:, cols]).astype(BF16)

    w_cols = pl.ds(pl.multiple_of(j * PROJ_TILE, PROJ_TILE), PROJ_TILE)
    o_ref[...] = jnp.dot(hn_ref[...], w_ref[:, w_cols], preferred_element_type=F32).astype(BF16)
    wff1_o[...] = wff1_ref[...].astype(BF16)
    wff2_o[...] = wff2_ref[...].astype(BF16)
    wout_o[...] = wout_ref[...].astype(BF16)


def _inproj(x, gain, w_in_tiles, w_ff1, w_ff2, w_out):
    def x_part(q):
        return pl.BlockSpec(
            (BATCH, TIME_TILE, X_PART),
            lambda i, j: (0, jnp.minimum(i + jnp.where(j > q, 1, 0), N_TILES - 1), q))

    def cast_rows(rows, width):
        return pl.BlockSpec((rows // N_CAST_STEPS, width),
                            lambda i, j: (i * N_PROJ_TILES + j, 0))

    return pl.pallas_call(
        _inproj_kernel,
        grid=(N_TILES, N_PROJ_TILES),
        in_specs=[
            x_part(0), x_part(1), x_part(2), x_part(3),
            pl.BlockSpec((1, D_MODEL), lambda i, j: (0, 0)),
            pl.BlockSpec((D_MODEL, D_IN), lambda i, j: (0, 0), pipeline_mode=pl.Buffered(1)),
            cast_rows(D_MODEL, D_FF), cast_rows(D_FF, D_MODEL), cast_rows(D_MODEL, D_MODEL),
        ],
        out_specs=[
            pl.BlockSpec((None, ROWS, PROJ_TILE), lambda i, j: (i, 0, j)),
            cast_rows(D_MODEL, D_FF), cast_rows(D_FF, D_MODEL), cast_rows(D_MODEL, D_MODEL),
        ],
        out_shape=[
            jax.ShapeDtypeStruct((N_TILES, ROWS, D_IN), BF16),
            jax.ShapeDtypeStruct((D_MODEL, D_FF), BF16),
            jax.ShapeDtypeStruct((D_FF, D_MODEL), BF16),
            jax.ShapeDtypeStruct((D_MODEL, D_MODEL), BF16),
        ],
        scratch_shapes=[pltpu.VMEM((ROWS, D_MODEL), BF16)],
        compiler_params=pltpu.CompilerParams(
            dimension_semantics=("arbitrary", "arbitrary"),
            vmem_limit_bytes=VMEM_LIMIT),
        name="inproj",
    )(x, x, x, x, gain, w_in_tiles, w_ff1, w_ff2, w_out)


def _sgu_steps(b, ub_ref, vb_ref, vgain_ref, w_heads, bs_ref, gain_b_ref, o_ref):
    rows = pl.ds(b * TIME_TILE, TIME_TILE)
    v = _rms(_gelu(vb_ref[rows, :].astype(F32)), vgain_ref[...]).astype(BF16)
    yield
    pieces = []
    for h in range(SGU_HEADS):
        cols = slice(h * SGU_HEAD_DIM, (h + 1) * SGU_HEAD_DIM)
        mixed = jnp.dot(w_heads[h], v[:, cols], preferred_element_type=F32) + bs_ref[h]
        pieces.append(_gelu(ub_ref[rows, cols].astype(F32)) * mixed)
        yield
    yb = jnp.concatenate(pieces, axis=-1)
    o_ref[rows, D_SSM:D_SSM + D_SGU] = _rms(yb, gain_b_ref[...]).astype(BF16)
    yield


def _drive(stream, n):
    for _ in range(n):
        try:
            next(stream)
        except StopIteration:
            return False
    return True


def _mixer_kernel(ua_ref, ga_ref, ub_ref, vb_ref, bd_ref, cd_ref, are_ref, aim_ref, dskip_ref,
                  vgain_ref, w_ref, bs_ref, gain_a_ref, gain_b_ref, o_ref,
                  state_ref, bu_ref, utb_ref, y_ref):
    @pl.when(pl.program_id(0) == 0)
    def _():
        state_ref[...] = jnp.zeros_like(state_ref)

    row_chunk = lax.broadcasted_iota(jnp.int32, (SGU_BLOCK, SGU_BLOCK), 0) // CHUNK
    col_chunk = lax.broadcasted_iota(jnp.int32, (SGU_BLOCK, SGU_BLOCK), 1) // CHUNK
    causal = col_chunk <= row_chunk
    w_heads = [jnp.where(causal, w_ref[h], 0.0).astype(BF16) for h in range(SGU_HEADS)]

    for b in range(BATCH):
        u_b = ua_ref[pl.ds(b * TIME_TILE, TIME_TILE), :].astype(F32)
        for s in range(N_CHUNKS):
            utb_ref[s, pl.ds(b, TIME_TILE, stride=BATCH), :] = u_b[:, _lanes(s)]

    half = CHUNK_ST // LANES
    readout_done = [False] * N_CHUNKS

    def input_matmul(kb):
        u16 = utb_ref[kb].astype(BF16)
        for jp in range(2 * CHUNK_ST // MXU_WIDTH):
            cols = slice(jp * MXU_WIDTH, (jp + 1) * MXU_WIDTH)
            bu_ref[kb % 2, :, cols] = jnp.dot(u16, bd_ref[kb, :, cols],
                                             preferred_element_type=F32)
            yield

    def readout(kb):
        hk = 2 * CHUNK_ST // READOUT_PARTS
        y_tb = dskip_ref[:, _lanes(kb)] * utb_ref[kb]
        for p in range(READOUT_PARTS):
            ks = slice(p * hk, (p + 1) * hk)
            y_tb = y_tb + jnp.dot(bu_ref[kb % 2, :, ks].astype(BF16), cd_ref[kb, ks, :],
                                  preferred_element_type=F32)
            yield
        utb_ref[kb] = y_tb
        readout_done[kb] = True
        yield

    def gate_steps(kb):
        for b in range(BATCH):
            rows = pl.ds(b * TIME_TILE, TIME_TILE)
            y = utb_ref[kb, pl.ds(b, TIME_TILE, stride=BATCH), :]
            y_ref[rows, _lanes(kb)] = _gelu(y) * _sigmoid(ga_ref[rows, _lanes(kb)].astype(F32))
            yield

    def scan(kb):
        a_re = [are_ref[kb, :, _lanes(j)] for j in range(half)]
        a_im = [aim_ref[kb, :, _lanes(j)] for j in range(half)]
        s_re = [state_ref[kb, :, _lanes(j)] for j in range(half)]
        s_im = [state_ref[kb, :, _lanes(j, CHUNK_ST)] for j in range(half)]
        buf = kb % 2
        for t in range(TIME_TILE):
            rows = slice(t * BATCH, (t + 1) * BATCH)
            b_re = [bu_ref[buf, rows, _lanes(j)] for j in range(half)]
            b_im = [bu_ref[buf, rows, _lanes(j, CHUNK_ST)] for j in range(half)]
            n_re = [a_re[j] * s_re[j] - a_im[j] * s_im[j] + b_re[j] for j in range(half)]
            n_im = [a_re[j] * s_im[j] + a_im[j] * s_re[j] + b_im[j] for j in range(half)]
            for j in range(half):
                bu_ref[buf, rows, _lanes(j)] = n_re[j]
                bu_ref[buf, rows, _lanes(j, CHUNK_ST)] = n_im[j]
            s_re, s_im = n_re, n_im
            if (t + 1) % SCAN_GROUP == 0:
                yield
        for j in range(half):
            state_ref[kb, :, _lanes(j)] = s_re[j]
            state_ref[kb, :, _lanes(j, CHUNK_ST)] = s_im[j]

    def chain(*gens):
        for g in gens:
            yield from g

    for _ in input_matmul(0):
        pass
    for kb in range(N_CHUNKS):
        mxu = chain(*([readout(kb - 1)] if kb > 0 else []),
                    *([input_matmul(kb + 1)] if kb + 1 < N_CHUNKS else []))
        side = chain(_sgu_steps(kb, ub_ref, vb_ref, vgain_ref, w_heads, bs_ref, gain_b_ref, o_ref),
                     *([gate_steps(kb - 1)] if kb > 0 else []))
        sgu_left = SGU_HEADS + 2
        for _ in scan(kb):
            _drive(mxu, MXU_PIECES_PER_GROUP)
            for _ in range(SIDE_STEPS_PER_GROUP):
                if sgu_left == 0 and kb > 0 and not readout_done[kb - 1]:
                    break
                _drive(side, 1)
                sgu_left = max(sgu_left - 1, 0)
        _drive(mxu, 1000)
        _drive(side, 1000)
    last = N_CHUNKS - 1
    _drive(readout(last), 1000)
    _drive(gate_steps(last), 1000)

    for b in range(BATCH):
        rows = pl.ds(b * TIME_TILE, TIME_TILE)
        o_ref[rows, 0:D_SSM] = _rms(y_ref[rows, :], gain_a_ref[...]).astype(BF16)


def _mixer(proj, bd, cd, a_re, a_im, d_skip, v_gain, w_s, bs, gain_a, gain_b):
    def const(shape):
        nd = len(shape)
        return pl.BlockSpec(shape, lambda i: (0,) * nd, pipeline_mode=pl.Buffered(1))

    def col(j):
        return pl.BlockSpec((None, ROWS, D_SSM), lambda i: (i, 0, j))

    return pl.pallas_call(
        _mixer_kernel,
        grid=(N_TILES,),
        in_specs=[
            col(0), col(1), col(2), col(3),
            const((N_CHUNKS, CHUNK_IN, 2 * CHUNK_ST)),
            const((N_CHUNKS, 2 * CHUNK_ST, CHUNK_IN)),
            const((N_CHUNKS, BATCH, CHUNK_ST)), const((N_CHUNKS, BATCH, CHUNK_ST)),
            const((1, D_SSM)), const((1, D_SGU)),
            const((SGU_HEADS, SGU_BLOCK, SGU_BLOCK)),
            const((SGU_HEADS, SGU_BLOCK, SGU_HEAD_DIM)),
            const((1, D_SSM)), const((1, D_SGU)),
        ],
        out_specs=pl.BlockSpec((None, ROWS, D_MODEL), lambda i: (i, 0, 0)),
        out_shape=jax.ShapeDtypeStruct((N_TILES, ROWS, D_MODEL), BF16),
        scratch_shapes=[
            pltpu.VMEM((N_CHUNKS, BATCH, 2 * CHUNK_ST), F32),
            pltpu.VMEM((2, ROWS, 2 * CHUNK_ST), F32),
            pltpu.VMEM((D_SSM // LANES, ROWS, LANES), F32),
            pltpu.VMEM((ROWS, D_SSM), F32),
        ],
        compiler_params=pltpu.CompilerParams(
            dimension_semantics=("arbitrary",),
            vmem_limit_bytes=VMEM_LIMIT),
        name="mixer",
    )(proj, proj, proj, proj, bd, cd, a_re, a_im, d_skip, v_gain, w_s, bs, gain_a, gain_b)


def _outproj_kernel(m_ref, w_ref, x_ref, o_ref):
    for c in range(D_MODEL // PROJ_TILE):
        cols = slice(c * PROJ_TILE, (c + 1) * PROJ_TILE)
        acc = jnp.dot(m_ref[...], w_ref[:, cols], preferred_element_type=F32)
        o_ref[:, cols] = x_ref[:, :, cols].reshape(ROWS, PROJ_TILE) + acc


def _outproj(mixed, w_out, x):
    return pl.pallas_call(
        _outproj_kernel,
        grid=(N_TILES,),
        in_specs=[
            pl.BlockSpec((None, ROWS, D_MODEL), lambda i: (i, 0, 0)),
            pl.BlockSpec((D_MODEL, D_MODEL), lambda i: (0, 0), pipeline_mode=pl.Buffered(1)),
            pl.BlockSpec((BATCH, TIME_TILE, D_MODEL), lambda i: (0, i, 0)),
        ],
        out_specs=pl.BlockSpec((None, ROWS, D_MODEL), lambda i: (i, 0, 0)),
        out_shape=jax.ShapeDtypeStruct((N_TILES, ROWS, D_MODEL), F32),
        compiler_params=pltpu.CompilerParams(
            dimension_semantics=("arbitrary",),
            vmem_limit_bytes=VMEM_LIMIT),
        name="outproj",
    )(mixed, w_out, x)


def _ffn_kernel(h_ref, gain_ref, w1_ref, w2_ref, fgain_ref, o_ref, hn_ref):
    k = pl.program_id(1)

    @pl.when(k == 0)
    def _():
        h = h_ref[...]
        hn_ref[...] = _rms(h, gain_ref[...]).astype(BF16)
        o_ref[...] = h.reshape(BATCH, TIME_TILE, D_MODEL)

    acts = []
    for p in range(FF_TILE // FF_PIECE):
        z = jnp.dot(hn_ref[...], w1_ref[:, p * FF_PIECE:(p + 1) * FF_PIECE],
                    preferred_element_type=F32)
        z = jnp.maximum(z, 0.0)
        acts.append((z * z).astype(BF16))
    a = jnp.concatenate(acts, axis=1)
    for c in range(D_MODEL // FF_PIECE):
        cols = slice(c * FF_PIECE, (c + 1) * FF_PIECE)
        upd = jnp.dot(a, w2_ref[:, cols], preferred_element_type=F32)
        o_ref[:, :, cols] += upd.reshape(BATCH, TIME_TILE, FF_PIECE)

    @pl.when(k == pl.num_programs(1) - 1)
    def _():
        acc = o_ref[...].reshape(ROWS, D_MODEL)
        o_ref[...] = _rms(acc, fgain_ref[...]).reshape(BATCH, TIME_TILE, D_MODEL)


def _ffn(h, gain, w1, w2, fgain):
    return pl.pallas_call(
        _ffn_kernel,
        grid=(N_TILES, D_FF // FF_TILE),
        in_specs=[
            pl.BlockSpec((None, ROWS, D_MODEL), lambda i, k: (i, 0, 0)),
            pl.BlockSpec((1, D_MODEL), lambda i, k: (0, 0)),
            pl.BlockSpec((D_MODEL, FF_TILE), lambda i, k: (0, k)),
            pl.BlockSpec((FF_TILE, D_MODEL), lambda i, k: (k, 0)),
            pl.BlockSpec((1, D_MODEL), lambda i, k: (0, 0)),
        ],
        out_specs=pl.BlockSpec((BATCH, TIME_TILE, D_MODEL), lambda i, k: (0, i, 0)),
        out_shape=jax.ShapeDtypeStruct((BATCH, SEQ, D_MODEL), F32),
        scratch_shapes=[pltpu.VMEM((ROWS, D_MODEL), BF16)],
        compiler_params=pltpu.CompilerParams(
            dimension_semantics=("arbitrary", "arbitrary"),
            vmem_limit_bytes=FFN_VMEM_LIMIT),
        name="ffn",
    )(h, gain, w1, w2, fgain)


def _ssm_params(a_re, a_im, log_dt, b_re, b_im, c_re, c_im):
    dt = jnp.exp(log_dt.astype(F32))[:, None]
    lam_re = jnp.minimum(a_re.astype(F32), -1e-4)
    lam_im = a_im.astype(F32)
    decay = jnp.exp(lam_re * dt)
    abar_re = decay * jnp.cos(lam_im * dt)
    abar_im = decay * jnp.sin(lam_im * dt)
    den = lam_re * lam_re + lam_im * lam_im
    num_re = abar_re - 1.0
    coef_re = (num_re * lam_re + abar_im * lam_im) / den
    coef_im = (abar_im * lam_re - num_re * lam_im) / den
    br = b_re.astype(F32)
    bi = b_im.astype(F32)
    bbar_re = coef_re[..., None] * br - coef_im[..., None] * bi
    bbar_im = coef_re[..., None] * bi + coef_im[..., None] * br

    g = GROUPS_PER_CHUNK

    def chunked(m):
        return jnp.swapaxes(m.reshape(N_CHUNKS, g, m.shape[1], m.shape[2]), 2, 3)

    b_parts = jnp.stack([chunked(bbar_re), chunked(bbar_im)], axis=3)
    b_rows = b_parts.reshape(N_CHUNKS, CHUNK_IN, 2, 1, SSM_STATE)
    b_tiled = jnp.broadcast_to(b_rows, (N_CHUNKS, CHUNK_IN, 2, g, SSM_STATE))
    row_group = lax.broadcasted_iota(jnp.int32, b_tiled.shape, 1) // SSM_GROUP
    col_group = lax.broadcasted_iota(jnp.int32, b_tiled.shape, 3)
    bd = jnp.where(row_group == col_group, b_tiled, 0.0)
    bd = bd.reshape(N_CHUNKS, CHUNK_IN, 2 * CHUNK_ST).astype(BF16)

    c_parts = jnp.stack([chunked(c_re.astype(F32)), chunked(-c_im.astype(F32))], axis=1)
    c_rows = c_parts.reshape(N_CHUNKS, 2 * CHUNK_ST, 1, SSM_GROUP)
    c_tiled = jnp.broadcast_to(c_rows, (N_CHUNKS, 2 * CHUNK_ST, g, SSM_GROUP))
    row_group = (lax.broadcasted_iota(jnp.int32, c_tiled.shape, 1) % CHUNK_ST) // SSM_STATE
    col_group = lax.broadcasted_iota(jnp.int32, c_tiled.shape, 2)
    cd = jnp.where(row_group == col_group, c_tiled, 0.0)
    cd = cd.reshape(N_CHUNKS, 2 * CHUNK_ST, CHUNK_IN).astype(BF16)

    def per_chunk(a):
        a = a.reshape(N_CHUNKS, 1, CHUNK_ST)
        return jnp.broadcast_to(a, (N_CHUNKS, BATCH, CHUNK_ST))

    return bd, cd, per_chunk(abar_re), per_chunk(abar_im)


def kernel(x, w_in, ssm_a_re, ssm_a_im, ssm_log_dt, ssm_b_re, ssm_b_im, ssm_c_re, ssm_c_im,
           ssm_d, sgu_v_gain, sgu_w, sgu_b, out_gain_ssm, out_gain_sgu, w_out,
           mix_norm_gain, ffn_norm_gain, w_ff1, w_ff2, final_norm_gain):
    row = lambda v: v.astype(F32).reshape(1, -1)
    assert x.shape == (BATCH, SEQ, D_MODEL) and w_in.shape[0] == 1, "single-layer block only"
    layer = 0
    bd, cd, a_re, a_im = _ssm_params(
        ssm_a_re[layer], ssm_a_im[layer], ssm_log_dt[layer], ssm_b_re[layer],
        ssm_b_im[layer], ssm_c_re[layer], ssm_c_im[layer])
    bs = jnp.broadcast_to(sgu_b[layer].astype(F32)[:, :, None],
                          (SGU_HEADS, SGU_BLOCK, SGU_HEAD_DIM))
    proj, w_ff1_16, w_ff2_16, w_out_16 = _inproj(
        x, row(mix_norm_gain[layer]), w_in[layer].astype(BF16), w_ff1[layer], w_ff2[layer],
        w_out[layer])
    mixed = _mixer(proj, bd, cd, a_re, a_im, row(ssm_d[layer]),
                   row(sgu_v_gain[layer]), sgu_w[layer].astype(F32), bs,
                   row(out_gain_ssm[layer]), row(out_gain_sgu[layer]))
    h_tiles = _outproj(mixed, w_out_16, x)
    return _ffn(h_tiles, row(ffn_norm_gain[layer]), w_ff1_16, w_ff2_16, row(final_norm_gain))
```

```python
import math

import jax
import jax.numpy as jnp
from jax import lax
from jax.experimental import pallas as pl
from jax.experimental.pallas import tpu as pltpu

D_MODEL = 2048
BATCH = 8
SEQ = 2048
CHUNK = 64
D_SSM = 1024
D_SGU = 1024
SSM_GROUP = 16
SSM_GROUPS = 64
SSM_STATE = 64
SGU_BLOCK = 128
SGU_HEADS = 8
SGU_HEAD_DIM = 128
D_FF = 4 * D_MODEL
D_IN = 2 * D_SSM + 2 * D_SGU
EPS = 1e-5

LANES = 128
MXU_WIDTH = 256

TIME_TILE = SGU_BLOCK
ROWS = BATCH * TIME_TILE
N_TILES = SEQ // TIME_TILE
GROUPS_PER_CHUNK = 8
N_CHUNKS = SSM_GROUPS // GROUPS_PER_CHUNK
CHUNK_IN = GROUPS_PER_CHUNK * SSM_GROUP
CHUNK_ST = GROUPS_PER_CHUNK * SSM_STATE
GATING_PLAN = ((0, 1), (2, 3), (4,), (5,), (6,), (7,), (), ())
SCAN_GROUP = 4
MXU_PIECES_PER_GROUP = 1
SIDE_STEPS_PER_GROUP = 1
FF_TILE = 1024
FF_PIECE = 512
PROJ_TILE = 1024
N_PROJ_TILES = D_IN // PROJ_TILE
X_PART = D_MODEL // N_PROJ_TILES
N_CAST_STEPS = N_TILES * N_PROJ_TILES
VMEM_LIMIT = 56 * 1024 * 1024
FFN_VMEM_LIMIT = 60 * 1024 * 1024

F32 = jnp.float32
BF16 = jnp.bfloat16


def _rms(x, gain):
    return x * lax.rsqrt(jnp.mean(x * x, axis=-1, keepdims=True) + EPS) * gain


def _gelu(x):
    c = math.sqrt(2.0 / math.pi)
    half_x = 0.5 * x
    return half_x + half_x * jnp.tanh(x * (c + (0.044715 * c) * (x * x)))


def _sigmoid(x):
    return 1.0 / (1.0 + jnp.exp(-x))


def _lanes(j, base=0):
    return slice(base + j * LANES, base + (j + 1) * LANES)


def _inproj_kernel(x0_ref, x1_ref, x2_ref, x3_ref, gain_ref, w_ref, wff1_ref, wff2_ref, wout_ref,
                   o_ref, wff1_o, wff2_o, wout_o, hn_ref):
    j = pl.program_id(1)

    @pl.when(j == 0)
    def _():
        parts = [r[...].reshape(ROWS, X_PART) for r in (x0_ref, x1_ref, x2_ref, x3_ref)]
        ssq = sum(jnp.sum(p * p, axis=-1, keepdims=True) for p in parts)
        scale = lax.rsqrt(ssq * (1.0 / D_MODEL) + EPS)
        for q, p in enumerate(parts):
            cols = slice(q * X_PART, (q + 1) * X_PART)
            hn_ref[:, cols] = (p * scale * gain_ref[:, cols]).astype(BF16)

    w_cols = pl.ds(pl.multiple_of(j * PROJ_TILE, PROJ_TILE), PROJ_TILE)
    o_ref[...] = jnp.dot(hn_ref[...], w_ref[:, w_cols], preferred_element_type=F32).astype(BF16)
    wff1_o[...] = wff1_ref[...].astype(BF16)
    wff2_o[...] = wff2_ref[...].astype(BF16)
    wout_o[...] = wout_ref[...].astype(BF16)


def _inproj(x, gain, w_in_tiles, w_ff1, w_ff2, w_out):
    def x_part(q):
        return pl.BlockSpec(
            (BATCH, TIME_TILE, X_PART),
            lambda i, j: (0, jnp.minimum(i + jnp.where(j > q, 1, 0), N_TILES - 1), q))

    def cast_rows(rows, width):
        return pl.BlockSpec((rows // N_CAST_STEPS, width),
                            lambda i, j: (i * N_PROJ_TILES + j, 0))

    return pl.pallas_call(
        _inproj_kernel,
        grid=(N_TILES, N_PROJ_TILES),
        in_specs=[
            x_part(0), x_part(1), x_part(2), x_part(3),
            pl.BlockSpec((1, D_MODEL), lambda i, j: (0, 0)),
            pl.BlockSpec((D_MODEL, D_IN), lambda i, j: (0, 0), pipeline_mode=pl.Buffered(1)),
            cast_rows(D_MODEL, D_FF), cast_rows(D_FF, D_MODEL), cast_rows(D_MODEL, D_MODEL),
        ],
        out_specs=[
            pl.BlockSpec((None, ROWS, PROJ_TILE), lambda i, j: (i, 0, j)),
            cast_rows(D_MODEL, D_FF), cast_rows(D_FF, D_MODEL), cast_rows(D_MODEL, D_MODEL),
        ],
        out_shape=[
            jax.ShapeDtypeStruct((N_TILES, ROWS, D_IN), BF16),
            jax.ShapeDtypeStruct((D_MODEL, D_FF), BF16),
            jax.ShapeDtypeStruct((D_FF, D_MODEL), BF16),
            jax.ShapeDtypeStruct((D_MODEL, D_MODEL), BF16),
        ],
        scratch_shapes=[pltpu.VMEM((ROWS, D_MODEL), BF16)],
        compiler_params=pltpu.CompilerParams(
            dimension_semantics=("arbitrary", "arbitrary"),
            vmem_limit_bytes=VMEM_LIMIT),
        name="inproj",
    )(x, x, x, x, gain, w_in_tiles, w_ff1, w_ff2, w_out)


def _sgu_steps(b, ub_ref, vb_ref, vgain_ref, w_heads, bs_ref, gain_b_ref, o_ref):
    rows = pl.ds(b * TIME_TILE, TIME_TILE)
    v = _rms(_gelu(vb_ref[rows, :].astype(F32)), vgain_ref[...]).astype(BF16)
    yield
    pieces = []
    for h in range(SGU_HEADS):
        cols = slice(h * SGU_HEAD_DIM, (h + 1) * SGU_HEAD_DIM)
        mixed = jnp.dot(w_heads[h], v[:, cols], preferred_element_type=F32) + bs_ref[h]
        pieces.append(_gelu(ub_ref[rows, cols].astype(F32)) * mixed)
        yield
    yb = jnp.concatenate(pieces, axis=-1)
    o_ref[rows, D_SSM:D_SSM + D_SGU] = _rms(yb, gain_b_ref[...]).astype(BF16)
    yield


def _drive(stream, n):
    for _ in range(n):
        try:
            next(stream)
        except StopIteration:
            return False
    return True


def _mixer_kernel(ua_ref, ga_ref, ub_ref, vb_ref, bd_ref, cd_ref, are_ref, aim_ref, dskip_ref,
                  vgain_ref, w_ref, bs_ref, gain_a_ref, gain_b_ref, o_ref,
                  state_ref, bu_ref, utb_ref, y_ref):
    @pl.when(pl.program_id(0) == 0)
    def _():
        state_ref[...] = jnp.zeros_like(state_ref)

    row_chunk = lax.broadcasted_iota(jnp.int32, (SGU_BLOCK, SGU_BLOCK), 0) // CHUNK
    col_chunk = lax.broadcasted_iota(jnp.int32, (SGU_BLOCK, SGU_BLOCK), 1) // CHUNK
    causal = col_chunk <= row_chunk
    w_heads = [jnp.where(causal, w_ref[h], 0.0).astype(BF16) for h in range(SGU_HEADS)]

    for b in range(BATCH):
        u_b = ua_ref[pl.ds(b * TIME_TILE, TIME_TILE), :].astype(F32)
        for s in range(N_CHUNKS):
            utb_ref[s, pl.ds(b, TIME_TILE, stride=BATCH), :] = u_b[:, _lanes(s)]

    half = CHUNK_ST // LANES
    readout_done = [False] * N_CHUNKS

    def input_matmul(kb):
        u16 = utb_ref[kb].astype(BF16)
        for jp in range(2 * CHUNK_ST // MXU_WIDTH):
            cols = slice(jp * MXU_WIDTH, (jp + 1) * MXU_WIDTH)
            bu_ref[kb % 2, :, cols] = jnp.dot(u16, bd_ref[kb, :, cols],
                                             preferred_element_type=F32)
            yield

    def readout(kb):
        halves = []
        for p in range(2):
            ks = slice(p * CHUNK_ST, (p + 1) * CHUNK_ST)
            halves.append(jnp.dot(bu_ref[kb % 2, :, ks].astype(BF16), cd_ref[kb, ks, :],
                                  preferred_element_type=F32))
            yield
        utb_ref[kb] = (dskip_ref[:, _lanes(kb)] * utb_ref[kb] + halves[0]) - halves[1]
        readout_done[kb] = True
        yield

    def gate_steps(kb):
        for b in range(BATCH):
            rows = pl.ds(b * TIME_TILE, TIME_TILE)
            y = utb_ref[kb, pl.ds(b, TIME_TILE, stride=BATCH), :]
            y_ref[rows, _lanes(kb)] = _gelu(y) * _sigmoid(ga_ref[rows, _lanes(kb)].astype(F32))
            yield

    def scan(kb):
        a_re = [are_ref[kb, :, _lanes(j)] for j in range(half)]
        a_im = [aim_ref[kb, :, _lanes(j)] for j in range(half)]
        s_re = [state_ref[kb, :, _lanes(j)] for j in range(half)]
        s_im = [state_ref[kb, :, _lanes(j, CHUNK_ST)] for j in range(half)]
        buf = kb % 2
        for t in range(TIME_TILE):
            rows = slice(t * BATCH, (t + 1) * BATCH)
            b_re = [bu_ref[buf, rows, _lanes(j)] for j in range(half)]
            b_im = [bu_ref[buf, rows, _lanes(j, CHUNK_ST)] for j in range(half)]
            n_re = [a_re[j] * s_re[j] - a_im[j] * s_im[j] + b_re[j] for j in range(half)]
            n_im = [a_re[j] * s_im[j] + a_im[j] * s_re[j] + b_im[j] for j in range(half)]
            for j in range(half):
                bu_ref[buf, rows, _lanes(j)] = n_re[j]
                bu_ref[buf, rows, _lanes(j, CHUNK_ST)] = n_im[j]
            s_re, s_im = n_re, n_im
            if (t + 1) % SCAN_GROUP == 0:
                yield
        for j in range(half):
            state_ref[kb, :, _lanes(j)] = s_re[j]
            state_ref[kb, :, _lanes(j, CHUNK_ST)] = s_im[j]

    def chain(*gens):
        for g in gens:
            yield from g

    for _ in input_matmul(0):
        pass
    for kb in range(N_CHUNKS):
        mxu = chain(*([readout(kb - 1)] if kb > 0 else []),
                    *([input_matmul(kb + 1)] if kb + 1 < N_CHUNKS else []))
        batches = GATING_PLAN[kb]
        side = chain(*[_sgu_steps(b, ub_ref, vb_ref, vgain_ref, w_heads, bs_ref, gain_b_ref, o_ref)
                       for b in batches],
                     *([gate_steps(kb - 1)] if kb > 0 else []))
        sgu_left = len(batches) * (SGU_HEADS + 2)
        for _ in scan(kb):
            _drive(mxu, MXU_PIECES_PER_GROUP)
            for _ in range(SIDE_STEPS_PER_GROUP):
                if sgu_left == 0 and kb > 0 and not readout_done[kb - 1]:
                    break
                _drive(side, 1)
                sgu_left = max(sgu_left - 1, 0)
        _drive(mxu, 1000)
        _drive(side, 1000)
    last = N_CHUNKS - 1
    _drive(readout(last), 1000)
    _drive(gate_steps(last), 1000)

    for b in range(BATCH):
        rows = pl.ds(b * TIME_TILE, TIME_TILE)
        o_ref[rows, 0:D_SSM] = _rms(y_ref[rows, :], gain_a_ref[...]).astype(BF16)


def _mixer(proj, bd, cd, a_re, a_im, d_skip, v_gain, w_s, bs, gain_a, gain_b):
    def const(shape):
        nd = len(shape)
        return pl.BlockSpec(shape, lambda i: (0,) * nd, pipeline_mode=pl.Buffered(1))

    def col(j):
        return pl.BlockSpec((None, ROWS, D_SSM), lambda i: (i, 0, j))

    return pl.pallas_call(
        _mixer_kernel,
        grid=(N_TILES,),
        in_specs=[
            col(0), col(1), col(2), col(3),
            const((N_CHUNKS, CHUNK_IN, 2 * CHUNK_ST)),
            const((N_CHUNKS, 2 * CHUNK_ST, CHUNK_IN)),
            const((N_CHUNKS, BATCH, CHUNK_ST)), const((N_CHUNKS, BATCH, CHUNK_ST)),
            const((1, D_SSM)), const((1, D_SGU)),
            const((SGU_HEADS, SGU_BLOCK, SGU_BLOCK)),
            const((SGU_HEADS, SGU_BLOCK, SGU_HEAD_DIM)),
            const((1, D_SSM)), const((1, D_SGU)),
        ],
        out_specs=pl.BlockSpec((None, ROWS, D_MODEL), lambda i: (i, 0, 0)),
        out_shape=jax.ShapeDtypeStruct((N_TILES, ROWS, D_MODEL), BF16),
        scratch_shapes=[
            pltpu.VMEM((N_CHUNKS, BATCH, 2 * CHUNK_ST), F32),
            pltpu.VMEM((2, ROWS, 2 * CHUNK_ST), F32),
            pltpu.VMEM((D_SSM // LANES, ROWS, LANES), F32),
            pltpu.VMEM((ROWS, D_SSM), F32),
        ],
        compiler_params=pltpu.CompilerParams(
            dimension_semantics=("arbitrary",),
            vmem_limit_bytes=VMEM_LIMIT),
        name="mixer",
    )(proj, proj, proj, proj, bd, cd, a_re, a_im, d_skip, v_gain, w_s, bs, gain_a, gain_b)


def _outproj_kernel(m_ref, w_ref, x_ref, o_ref):
    for c in range(D_MODEL // PROJ_TILE):
        cols = slice(c * PROJ_TILE, (c + 1) * PROJ_TILE)
        acc = jnp.dot(m_ref[...], w_ref[:, cols], preferred_element_type=F32)
        o_ref[:, cols] = x_ref[:, :, cols].reshape(ROWS, PROJ_TILE) + acc


def _outproj(mixed, w_out, x):
    return pl.pallas_call(
        _outproj_kernel,
        grid=(N_TILES,),
        in_specs=[
            pl.BlockSpec((None, ROWS, D_MODEL), lambda i: (i, 0, 0)),
            pl.BlockSpec((D_MODEL, D_MODEL), lambda i: (0, 0), pipeline_mode=pl.Buffered(1)),
            pl.BlockSpec((BATCH, TIME_TILE, D_MODEL), lambda i: (0, i, 0)),
        ],
        out_specs=pl.BlockSpec((None, ROWS, D_MODEL), lambda i: (i, 0, 0)),
        out_shape=jax.ShapeDtypeStruct((N_TILES, ROWS, D_MODEL), F32),
        compiler_params=pltpu.CompilerParams(
            dimension_semantics=("arbitrary",),
            vmem_limit_bytes=VMEM_LIMIT),
        name="outproj",
    )(mixed, w_out, x)


def _ffn_kernel(h_ref, gain_ref, w1_ref, w2_ref, fgain_ref, o_ref, hn_ref):
    k = pl.program_id(1)

    @pl.when(k == 0)
    def _():
        h = h_ref[...]
        hn_ref[...] = _rms(h, gain_ref[...]).astype(BF16)
        o_ref[...] = h.reshape(BATCH, TIME_TILE, D_MODEL)

    acts = []
    for p in range(FF_TILE // FF_PIECE):
        z = jnp.dot(hn_ref[...], w1_ref[:, p * FF_PIECE:(p + 1) * FF_PIECE],
                    preferred_element_type=F32)
        z = jnp.maximum(z, 0.0)
        acts.append((z * z).astype(BF16))
    a = jnp.concatenate(acts, axis=1)
    for c in range(D_MODEL // FF_PIECE):
        cols = slice(c * FF_PIECE, (c + 1) * FF_PIECE)
        upd = jnp.dot(a, w2_ref[:, cols], preferred_element_type=F32)
        o_ref[:, :, cols] += upd.reshape(BATCH, TIME_TILE, FF_PIECE)

    @pl.when(k == pl.num_programs(1) - 1)
    def _():
        acc = o_ref[...].reshape(ROWS, D_MODEL)
        o_ref[...] = _rms(acc, fgain_ref[...]).reshape(BATCH, TIME_TILE, D_MODEL)


def _ffn(h, gain, w1, w2, fgain):
    return pl.pallas_call(
        _ffn_kernel,
        grid=(N_TILES, D_FF // FF_TILE),
        in_specs=[
            pl.BlockSpec((None, ROWS, D_MODEL), lambda i, k: (i, 0, 0)),
            pl.BlockSpec((1, D_MODEL), lambda i, k: (0, 0)),
            pl.BlockSpec((D_MODEL, FF_TILE), lambda i, k: (0, k)),
            pl.BlockSpec((FF_TILE, D_MODEL), lambda i, k: (k, 0)),
            pl.BlockSpec((1, D_MODEL), lambda i, k: (0, 0)),
        ],
        out_specs=pl.BlockSpec((BATCH, TIME_TILE, D_MODEL), lambda i, k: (0, i, 0)),
        out_shape=jax.ShapeDtypeStruct((BATCH, SEQ, D_MODEL), F32),
        scratch_shapes=[pltpu.VMEM((ROWS, D_MODEL), BF16)],
        compiler_params=pltpu.CompilerParams(
            dimension_semantics=("arbitrary", "arbitrary"),
            vmem_limit_bytes=FFN_VMEM_LIMIT),
        name="ffn",
    )(h, gain, w1, w2, fgain)


def _ssm_params(a_re, a_im, log_dt, b_re, b_im, c_re, c_im):
    dt = jnp.exp(log_dt.astype(F32))[:, None]
    lam_re = jnp.minimum(a_re.astype(F32), -1e-4)
    lam_im = a_im.astype(F32)
    decay = jnp.exp(lam_re * dt)
    abar_re = decay * jnp.cos(lam_im * dt)
    abar_im = decay * jnp.sin(lam_im * dt)
    den = lam_re * lam_re + lam_im * lam_im
    num_re = abar_re - 1.0
    coef_re = (num_re * lam_re + abar_im * lam_im) / den
    coef_im = (abar_im * lam_re - num_re * lam_im) / den
    br = b_re.astype(F32)
    bi = b_im.astype(F32)
    bbar_re = coef_re[..., None] * br - coef_im[..., None] * bi
    bbar_im = coef_re[..., None] * bi + coef_im[..., None] * br

    g = GROUPS_PER_CHUNK

    def chunked(m):
        return jnp.swapaxes(m.reshape(N_CHUNKS, g, m.shape[1], m.shape[2]), 2, 3)

    b_parts = jnp.stack([chunked(bbar_re), chunked(bbar_im)], axis=3)
    b_rows = b_parts.reshape(N_CHUNKS, CHUNK_IN, 2, 1, SSM_STATE)
    b_tiled = jnp.broadcast_to(b_rows, (N_CHUNKS, CHUNK_IN, 2, g, SSM_STATE))
    row_group = lax.broadcasted_iota(jnp.int32, b_tiled.shape, 1) // SSM_GROUP
    col_group = lax.broadcasted_iota(jnp.int32, b_tiled.shape, 3)
    bd = jnp.where(row_group == col_group, b_tiled, 0.0)
    bd = bd.reshape(N_CHUNKS, CHUNK_IN, 2 * CHUNK_ST).astype(BF16)

    c_parts = jnp.stack([chunked(c_re.astype(F32)), chunked(c_im.astype(F32))], axis=1)
    c_rows = c_parts.reshape(N_CHUNKS, 2 * CHUNK_ST, 1, SSM_GROUP)
    c_tiled = jnp.broadcast_to(c_rows, (N_CHUNKS, 2 * CHUNK_ST, g, SSM_GROUP))
    row_group = (lax.broadcasted_iota(jnp.int32, c_tiled.shape, 1) % CHUNK_ST) // SSM_STATE
    col_group = lax.broadcasted_iota(jnp.int32, c_tiled.shape, 2)
    cd = jnp.where(row_group == col_group, c_tiled, 0.0)
    cd = cd.reshape(N_CHUNKS, 2 * CHUNK_ST, CHUNK_IN).astype(BF16)

    def per_chunk(a):
        a = a.reshape(N_CHUNKS, 1, CHUNK_ST)
        return jnp.broadcast_to(a, (N_CHUNKS, BATCH, CHUNK_ST))

    return bd, cd, per_chunk(abar_re), per_chunk(abar_im)


def kernel(x, w_in, ssm_a_re, ssm_a_im, ssm_log_dt, ssm_b_re, ssm_b_im, ssm_c_re, ssm_c_im,
           ssm_d, sgu_v_gain, sgu_w, sgu_b, out_gain_ssm, out_gain_sgu, w_out,
           mix_norm_gain, ffn_norm_gain, w_ff1, w_ff2, final_norm_gain):
    row = lambda v: v.astype(F32).reshape(1, -1)
    assert x.shape == (BATCH, SEQ, D_MODEL) and w_in.shape[0] == 1, "single-layer block only"
    layer = 0
    bd, cd, a_re, a_im = _ssm_params(
        ssm_a_re[layer], ssm_a_im[layer], ssm_log_dt[layer], ssm_b_re[layer],
        ssm_b_im[layer], ssm_c_re[layer], ssm_c_im[layer])
    bs = jnp.broadcast_to(sgu_b[layer].astype(F32)[:, :, None],
                          (SGU_HEADS, SGU_BLOCK, SGU_HEAD_DIM))
    proj, w_ff1_16, w_ff2_16, w_out_16 = _inproj(
        x, row(mix_norm_gain[layer]), w_in[layer].astype(BF16), w_ff1[layer], w_ff2[layer],
        w_out[layer])
    mixed = _mixer(proj, bd, cd, a_re, a_im, row(ssm_d[layer]),
                   row(sgu_v_gain[layer]), sgu_w[layer].astype(F32), bs,
                   row(out_gain_ssm[layer]), row(out_gain_sgu[layer]))
    h_tiles = _outproj(mixed, w_out_16, x)
    return _ffn(h_tiles, row(ffn_norm_gain[layer]), w_ff1_16, w_ff2_16, row(final_norm_gain))
```

```python
import math

import jax
import jax.numpy as jnp
from jax import lax
from jax.experimental import pallas as pl
from jax.experimental.pallas import tpu as pltpu

D_MODEL = 2048
BATCH = 8
SEQ = 2048
CHUNK = 64
D_SSM = 1024
D_SGU = 1024
SSM_GROUP = 16
SSM_GROUPS = 64
SSM_STATE = 64
SGU_BLOCK = 128
SGU_HEADS = 8
SGU_HEAD_DIM = 128
D_FF = 4 * D_MODEL
D_IN = 2 * D_SSM + 2 * D_SGU
EPS = 1e-5

LANES = 128
MXU_WIDTH = 256

TIME_TILE = SGU_BLOCK
ROWS = BATCH * TIME_TILE
N_TILES = SEQ // TIME_TILE
GROUPS_PER_CHUNK = 8
N_CHUNKS = SSM_GROUPS // GROUPS_PER_CHUNK
CHUNK_IN = GROUPS_PER_CHUNK * SSM_GROUP
CHUNK_ST = GROUPS_PER_CHUNK * SSM_STATE
GATING_PLAN = ((0, 1), (2, 3), (4,), (5,), (6,), (7,), (), ())
SCAN_GROUP = 4
MXU_PIECES_PER_GROUP = 1
SIDE_STEPS_PER_GROUP = 1
FF_TILE = 1024
FF_PIECE = 512
PROJ_TILE = 1024
N_PROJ_TILES = D_IN // PROJ_TILE
X_PART = D_MODEL // N_PROJ_TILES
N_CAST_STEPS = N_TILES * N_PROJ_TILES
VMEM_LIMIT = 56 * 1024 * 1024
FFN_VMEM_LIMIT = 60 * 1024 * 1024

F32 = jnp.float32
BF16 = jnp.bfloat16


def _rms(x, gain):
    return x * lax.rsqrt(jnp.mean(x * x, axis=-1, keepdims=True) + EPS) * gain


def _gelu(x):
    c = math.sqrt(2.0 / math.pi)
    half_x = 0.5 * x
    return half_x + half_x * jnp.tanh(x * (c + (0.044715 * c) * (x * x)))


def _sigmoid(x):
    return 1.0 / (1.0 + jnp.exp(-x))


def _lanes(j, base=0):
    return slice(base + j * LANES, base + (j + 1) * LANES)


def _inproj_kernel(x0_ref, x1_ref, x2_ref, x3_ref, gain_ref, w_ref, wff1_ref, wff2_ref, wout_ref,
                   o_ref, wff1_o, wff2_o, wout_o, hn_ref):
    j = pl.program_id(1)

    @pl.when(j == 0)
    def _():
        parts = [r[...].reshape(ROWS, X_PART) for r in (x0_ref, x1_ref, x2_ref, x3_ref)]
        ssq = sum(jnp.sum(p * p, axis=-1, keepdims=True) for p in parts)
        scale = lax.rsqrt(ssq * (1.0 / D_MODEL) + EPS)
        for q, p in enumerate(parts):
            cols = slice(q * X_PART, (q + 1) * X_PART)
            hn_ref[:, cols] = (p * scale * gain_ref[:, cols]).astype(BF16)

    w_cols = pl.ds(pl.multiple_of(j * PROJ_TILE, PROJ_TILE), PROJ_TILE)
    o_ref[...] = jnp.dot(hn_ref[...], w_ref[:, w_cols], preferred_element_type=F32).astype(BF16)
    wff1_o[...] = wff1_ref[...].astype(BF16)
    wff2_o[...] = wff2_ref[...].astype(BF16)
    wout_o[...] = wout_ref[...].astype(BF16)


def _inproj(x, gain, w_in_tiles, w_ff1, w_ff2, w_out):
    def x_part(q):
        return pl.BlockSpec(
            (BATCH, TIME_TILE, X_PART),
            lambda i, j: (0, jnp.minimum(i + jnp.where(j > q, 1, 0), N_TILES - 1), q))

    def cast_rows(rows, width):
        return pl.BlockSpec((rows // N_CAST_STEPS, width),
                            lambda i, j: (i * N_PROJ_TILES + j, 0))

    return pl.pallas_call(
        _inproj_kernel,
        grid=(N_TILES, N_PROJ_TILES),
        in_specs=[
            x_part(0), x_part(1), x_part(2), x_part(3),
            pl.BlockSpec((1, D_MODEL), lambda i, j: (0, 0)),
            pl.BlockSpec((D_MODEL, D_IN), lambda i, j: (0, 0), pipeline_mode=pl.Buffered(1)),
            cast_rows(D_MODEL, D_FF), cast_rows(D_FF, D_MODEL), cast_rows(D_MODEL, D_MODEL),
        ],
        out_specs=[
            pl.BlockSpec((None, ROWS, PROJ_TILE), lambda i, j: (i, 0, j)),
            cast_rows(D_MODEL, D_FF), cast_rows(D_FF, D_MODEL), cast_rows(D_MODEL, D_MODEL),
        ],
        out_shape=[
            jax.ShapeDtypeStruct((N_TILES, ROWS, D_IN), BF16),
            jax.ShapeDtypeStruct((D_MODEL, D_FF), BF16),
            jax.ShapeDtypeStruct((D_FF, D_MODEL), BF16),
            jax.ShapeDtypeStruct((D_MODEL, D_MODEL), BF16),
        ],
        scratch_shapes=[pltpu.VMEM((ROWS, D_MODEL), BF16)],
        compiler_params=pltpu.CompilerParams(
            dimension_semantics=("arbitrary", "arbitrary"),
            vmem_limit_bytes=VMEM_LIMIT),
        name="inproj",
    )(x, x, x, x, gain, w_in_tiles, w_ff1, w_ff2, w_out)


def _sgu_steps(b, ub_ref, vb_ref, vgain_ref, w_heads, bs_ref, gain_b_ref, o_ref):
    rows = pl.ds(b * TIME_TILE, TIME_TILE)
    v = _rms(_gelu(vb_ref[rows, :].astype(F32)), vgain_ref[...]).astype(BF16)
    yield
    pieces = []
    for h in range(SGU_HEADS):
        cols = slice(h * SGU_HEAD_DIM, (h + 1) * SGU_HEAD_DIM)
        mixed = jnp.dot(w_heads[h], v[:, cols], preferred_element_type=F32) + bs_ref[h]
        pieces.append(_gelu(ub_ref[rows, cols].astype(F32)) * mixed)
        yield
    yb = jnp.concatenate(pieces, axis=-1)
    o_ref[rows, D_SSM:D_SSM + D_SGU] = _rms(yb, gain_b_ref[...]).astype(BF16)
    yield


def _drive(stream, n):
    for _ in range(n):
        try:
            next(stream)
        except StopIteration:
            return False
    return True


def _mixer_kernel(ua_ref, ga_ref, ub_ref, vb_ref, bd_ref, cd_ref, are_ref, aim_ref, dskip_ref,
                  vgain_ref, w_ref, bs_ref, gain_a_ref, gain_b_ref, o_ref,
                  state_ref, bu_ref, utb_ref, y_ref):
    @pl.when(pl.program_id(0) == 0)
    def _():
        state_ref[...] = jnp.zeros_like(state_ref)

    row_chunk = lax.broadcasted_iota(jnp.int32, (SGU_BLOCK, SGU_BLOCK), 0) // CHUNK
    col_chunk = lax.broadcasted_iota(jnp.int32, (SGU_BLOCK, SGU_BLOCK), 1) // CHUNK
    causal = col_chunk <= row_chunk
    w_heads = [jnp.where(causal, w_ref[h], 0.0).astype(BF16) for h in range(SGU_HEADS)]

    for b in range(BATCH):
        u_b = ua_ref[pl.ds(b * TIME_TILE, TIME_TILE), :].astype(F32)
        for s in range(N_CHUNKS):
            utb_ref[s, pl.ds(b, TIME_TILE, stride=BATCH), :] = u_b[:, _lanes(s)]

    half = CHUNK_ST // LANES
    readout_done = [False] * N_CHUNKS

    def input_matmul(kb):
        u16 = utb_ref[kb].astype(BF16)
        for jp in range(2 * CHUNK_ST // MXU_WIDTH):
            cols = slice(jp * MXU_WIDTH, (jp + 1) * MXU_WIDTH)
            bu_ref[kb % 2, :, cols] = jnp.dot(u16, bd_ref[kb, :, cols],
                                             preferred_element_type=F32)
            yield

    def readout(kb):
        halves = []
        for p in range(2):
            ks = slice(p * CHUNK_ST, (p + 1) * CHUNK_ST)
            halves.append(jnp.dot(bu_ref[kb % 2, :, ks].astype(BF16), cd_ref[kb, ks, :],
                                  preferred_element_type=F32))
            yield
        utb_ref[kb] = (dskip_ref[:, _lanes(kb)] * utb_ref[kb] + halves[0]) - halves[1]
        readout_done[kb] = True
        yield

    def gate_steps(kb):
        for b in range(BATCH):
            rows = pl.ds(b * TIME_TILE, TIME_TILE)
            y = utb_ref[kb, pl.ds(b, TIME_TILE, stride=BATCH), :]
            y_ref[rows, _lanes(kb)] = _gelu(y) * _sigmoid(ga_ref[rows, _lanes(kb)].astype(F32))
            yield

    def scan(kb):
        a_re = [are_ref[kb, :, _lanes(j)] for j in range(half)]
        a_im = [aim_ref[kb, :, _lanes(j)] for j in range(half)]
        s_re = [state_ref[kb, :, _lanes(j)] for j in range(half)]
        s_im = [state_ref[kb, :, _lanes(j, CHUNK_ST)] for j in range(half)]
        buf = kb % 2
        for t in range(TIME_TILE):
            rows = slice(t * BATCH, (t + 1) * BATCH)
            b_re = [bu_ref[buf, rows, _lanes(j)] for j in range(half)]
            b_im = [bu_ref[buf, rows, _lanes(j, CHUNK_ST)] for j in range(half)]
            n_re = [a_re[j] * s_re[j] - a_im[j] * s_im[j] + b_re[j] for j in range(half)]
            n_im = [a_re[j] * s_im[j] + a_im[j] * s_re[j] + b_im[j] for j in range(half)]
            for j in range(half):
                bu_ref[buf, rows, _lanes(j)] = n_re[j]
                bu_ref[buf, rows, _lanes(j, CHUNK_ST)] = n_im[j]
            s_re, s_im = n_re, n_im
            if (t + 1) % SCAN_GROUP == 0:
                yield
        for j in range(half):
            state_ref[kb, :, _lanes(j)] = s_re[j]
            state_ref[kb, :, _lanes(j, CHUNK_ST)] = s_im[j]

    def chain(*gens):
        for g in gens:
            yield from g

    for _ in input_matmul(0):
        pass
    for kb in range(N_CHUNKS):
        mxu = chain(*([readout(kb - 1)] if kb > 0 else []),
                    *([input_matmul(kb + 1)] if kb + 1 < N_CHUNKS else []))
        batches = GATING_PLAN[kb]
        side = chain(*[_sgu_steps(b, ub_ref, vb_ref, vgain_ref, w_heads, bs_ref, gain_b_ref, o_ref)
                       for b in batches],
                     *([gate_steps(kb - 1)] if kb > 0 else []))
        sgu_left = len(batches) * (SGU_HEADS + 2)
        for _ in scan(kb):
            _drive(mxu, MXU_PIECES_PER_GROUP)
            for _ in range(SIDE_STEPS_PER_GROUP):
                if sgu_left == 0 and kb > 0 and not readout_done[kb - 1]:
                    break
                _drive(side, 1)
                sgu_left = max(sgu_left - 1, 0)
        _drive(mxu, 1000)
        _drive(side, 1000)
    last = N_CHUNKS - 1
    _drive(readout(last), 1000)
    _drive(gate_steps(last), 1000)

    for b in range(BATCH):
        rows = pl.ds(b * TIME_TILE, TIME_TILE)
        o_ref[rows, 0:D_SSM] = _rms(y_ref[rows, :], gain_a_ref[...]).astype(BF16)


def _mixer(proj, bd, cd, a_re, a_im, d_skip, v_gain, w_s, bs, gain_a, gain_b):
    def const(shape):
        nd = len(shape)
        return pl.BlockSpec(shape, lambda i: (0,) * nd, pipeline_mode=pl.Buffered(1))

    def col(j):
        return pl.BlockSpec((None, ROWS, D_SSM), lambda i: (i, 0, j))

    return pl.pallas_call(
        _mixer_kernel,
        grid=(N_TILES,),
        in_specs=[
            col(0), col(1), col(2), col(3),
            const((N_CHUNKS, CHUNK_IN, 2 * CHUNK_ST)),
            const((N_CHUNKS, 2 * CHUNK_ST, CHUNK_IN)),
            const((N_CHUNKS, BATCH, CHUNK_ST)), const((N_CHUNKS, BATCH, CHUNK_ST)),
            const((1, D_SSM)), const((1, D_SGU)),
            const((SGU_HEADS, SGU_BLOCK, SGU_BLOCK)),
            const((SGU_HEADS, SGU_BLOCK, SGU_HEAD_DIM)),
            const((1, D_SSM)), const((1, D_SGU)),
        ],
        out_specs=pl.BlockSpec((None, ROWS, D_MODEL), lambda i: (i, 0, 0)),
        out_shape=jax.ShapeDtypeStruct((N_TILES, ROWS, D_MODEL), BF16),
        scratch_shapes=[
            pltpu.VMEM((N_CHUNKS, BATCH, 2 * CHUNK_ST), F32),
            pltpu.VMEM((2, ROWS, 2 * CHUNK_ST), F32),
            pltpu.VMEM((D_SSM // LANES, ROWS, LANES), F32),
            pltpu.VMEM((ROWS, D_SSM), F32),
        ],
        compiler_params=pltpu.CompilerParams(
            dimension_semantics=("arbitrary",),
            vmem_limit_bytes=VMEM_LIMIT),
        name="mixer",
    )(proj, proj, proj, proj, bd, cd, a_re, a_im, d_skip, v_gain, w_s, bs, gain_a, gain_b)


def _outproj_kernel(m_ref, w_ref, x_ref, o_ref):
    for c in range(D_MODEL // PROJ_TILE):
        cols = slice(c * PROJ_TILE, (c + 1) * PROJ_TILE)
        acc = jnp.dot(m_ref[...], w_ref[:, cols], preferred_element_type=F32)
        o_ref[:, cols] = x_ref[:, :, cols].reshape(ROWS, PROJ_TILE) + acc


def _outproj(mixed, w_out, x):
    return pl.pallas_call(
        _outproj_kernel,
        grid=(N_TILES,),
        in_specs=[
            pl.BlockSpec((None, ROWS, D_MODEL), lambda i: (i, 0, 0)),
            pl.BlockSpec((D_MODEL, D_MODEL), lambda i: (0, 0), pipeline_mode=pl.Buffered(1)),
            pl.BlockSpec((BATCH, TIME_TILE, D_MODEL), lambda i: (0, i, 0)),
        ],
        out_specs=pl.BlockSpec((None, ROWS, D_MODEL), lambda i: (i, 0, 0)),
        out_shape=jax.ShapeDtypeStruct((N_TILES, ROWS, D_MODEL), F32),
        compiler_params=pltpu.CompilerParams(
            dimension_semantics=("arbitrary",),
            vmem_limit_bytes=VMEM_LIMIT),
        name="outproj",
    )(mixed, w_out, x)


def _ffn_kernel(h_hbm, gain_ref, w1_ref, w2_ref, fgain_ref, o_ref, hn_ref, h_buf, h_sem):
    i = pl.program_id(0)
    k = pl.program_id(1)

    def h_copy(tile):
        return pltpu.make_async_copy(h_hbm.at[tile], h_buf, h_sem)

    @pl.when((i == 0) & (k == 0))
    def _():
        h_copy(0).start()

    @pl.when(k == 0)
    def _():
        h_copy(i).wait()
        h = h_buf[...]
        hn_ref[...] = _rms(h, gain_ref[...]).astype(BF16)
        o_ref[...] = h.reshape(BATCH, TIME_TILE, D_MODEL)

    @pl.when((k == 1) & (i + 1 < N_TILES))
    def _():
        h_copy(i + 1).start()

    acts = []
    for p in range(FF_TILE // FF_PIECE):
        z = jnp.dot(hn_ref[...], w1_ref[:, p * FF_PIECE:(p + 1) * FF_PIECE],
                    preferred_element_type=F32)
        z = jnp.maximum(z, 0.0)
        acts.append((z * z).astype(BF16))
    a = jnp.concatenate(acts, axis=1)
    for c in range(D_MODEL // FF_PIECE):
        cols = slice(c * FF_PIECE, (c + 1) * FF_PIECE)
        upd = jnp.dot(a, w2_ref[:, cols], preferred_element_type=F32)
        o_ref[:, :, cols] += upd.reshape(BATCH, TIME_TILE, FF_PIECE)

    @pl.when(k == pl.num_programs(1) - 1)
    def _():
        acc = o_ref[...].reshape(ROWS, D_MODEL)
        o_ref[...] = _rms(acc, fgain_ref[...]).reshape(BATCH, TIME_TILE, D_MODEL)


def _ffn(h, gain, w1, w2, fgain):
    return pl.pallas_call(
        _ffn_kernel,
        grid=(N_TILES, D_FF // FF_TILE),
        in_specs=[
            pl.BlockSpec(memory_space=pl.ANY),
            pl.BlockSpec((1, D_MODEL), lambda i, k: (0, 0)),
            pl.BlockSpec((D_MODEL, FF_TILE), lambda i, k: (0, k)),
            pl.BlockSpec((FF_TILE, D_MODEL), lambda i, k: (k, 0)),
            pl.BlockSpec((1, D_MODEL), lambda i, k: (0, 0)),
        ],
        out_specs=pl.BlockSpec((BATCH, TIME_TILE, D_MODEL), lambda i, k: (0, i, 0)),
        out_shape=jax.ShapeDtypeStruct((BATCH, SEQ, D_MODEL), F32),
        scratch_shapes=[pltpu.VMEM((ROWS, D_MODEL), BF16),
                        pltpu.VMEM((ROWS, D_MODEL), F32),
                        pltpu.SemaphoreType.DMA(())],
        compiler_params=pltpu.CompilerParams(
            dimension_semantics=("arbitrary", "arbitrary"),
            vmem_limit_bytes=FFN_VMEM_LIMIT),
        name="ffn",
    )(h, gain, w1, w2, fgain)


def _ssm_params(a_re, a_im, log_dt, b_re, b_im, c_re, c_im):
    dt = jnp.exp(log_dt.astype(F32))[:, None]
    lam_re = jnp.minimum(a_re.astype(F32), -1e-4)
    lam_im = a_im.astype(F32)
    decay = jnp.exp(lam_re * dt)
    abar_re = decay * jnp.cos(lam_im * dt)
    abar_im = decay * jnp.sin(lam_im * dt)
    den = lam_re * lam_re + lam_im * lam_im
    num_re = abar_re - 1.0
    coef_re = (num_re * lam_re + abar_im * lam_im) / den
    coef_im = (abar_im * lam_re - num_re * lam_im) / den
    br = b_re.astype(F32)
    bi = b_im.astype(F32)
    bbar_re = coef_re[..., None] * br - coef_im[..., None] * bi
    bbar_im = coef_re[..., None] * bi + coef_im[..., None] * br

    g = GROUPS_PER_CHUNK

    def chunked(m):
        return jnp.swapaxes(m.reshape(N_CHUNKS, g, m.shape[1], m.shape[2]), 2, 3)

    b_parts = jnp.stack([chunked(bbar_re), chunked(bbar_im)], axis=3)
    b_rows = b_parts.reshape(N_CHUNKS, CHUNK_IN, 2, 1, SSM_STATE)
    b_tiled = jnp.broadcast_to(b_rows, (N_CHUNKS, CHUNK_IN, 2, g, SSM_STATE))
    row_group = lax.broadcasted_iota(jnp.int32, b_tiled.shape, 1) // SSM_GROUP
    col_group = lax.broadcasted_iota(jnp.int32, b_tiled.shape, 3)
    bd = jnp.where(row_group == col_group, b_tiled, 0.0)
    bd = bd.reshape(N_CHUNKS, CHUNK_IN, 2 * CHUNK_ST).astype(BF16)

    c_parts = jnp.stack([chunked(c_re.astype(F32)), chunked(c_im.astype(F32))], axis=1)
    c_rows = c_parts.reshape(N_CHUNKS, 2 * CHUNK_ST, 1, SSM_GROUP)
    c_tiled = jnp.broadcast_to(c_rows, (N_CHUNKS, 2 * CHUNK_ST, g, SSM_GROUP))
    row_group = (lax.broadcasted_iota(jnp.int32, c_tiled.shape, 1) % CHUNK_ST) // SSM_STATE
    col_group = lax.broadcasted_iota(jnp.int32, c_tiled.shape, 2)
    cd = jnp.where(row_group == col_group, c_tiled, 0.0)
    cd = cd.reshape(N_CHUNKS, 2 * CHUNK_ST, CHUNK_IN).astype(BF16)

    def per_chunk(a):
        a = a.reshape(N_CHUNKS, 1, CHUNK_ST)
        return jnp.broadcast_to(a, (N_CHUNKS, BATCH, CHUNK_ST))

    return bd, cd, per_chunk(abar_re), per_chunk(abar_im)


def kernel(x, w_in, ssm_a_re, ssm_a_im, ssm_log_dt, ssm_b_re, ssm_b_im, ssm_c_re, ssm_c_im,
           ssm_d, sgu_v_gain, sgu_w, sgu_b, out_gain_ssm, out_gain_sgu, w_out,
           mix_norm_gain, ffn_norm_gain, w_ff1, w_ff2, final_norm_gain):
    row = lambda v: v.astype(F32).reshape(1, -1)
    assert x.shape == (BATCH, SEQ, D_MODEL) and w_in.shape[0] == 1, "single-layer block only"
    layer = 0
    bd, cd, a_re, a_im = _ssm_params(
        ssm_a_re[layer], ssm_a_im[layer], ssm_log_dt[layer], ssm_b_re[layer],
        ssm_b_im[layer], ssm_c_re[layer], ssm_c_im[layer])
    bs = jnp.broadcast_to(sgu_b[layer].astype(F32)[:, :, None],
                          (SGU_HEADS, SGU_BLOCK, SGU_HEAD_DIM))
    proj, w_ff1_16, w_ff2_16, w_out_16 = _inproj(
        x, row(mix_norm_gain[layer]), w_in[layer].astype(BF16), w_ff1[layer], w_ff2[layer],
        w_out[layer])
    mixed = _mixer(proj, bd, cd, a_re, a_im, row(ssm_d[layer]),
                   row(sgu_v_gain[layer]), sgu_w[layer].astype(F32), bs,
                   row(out_gain_ssm[layer]), row(out_gain_sgu[layer]))
    h_tiles = _outproj(mixed, w_out_16, x)
    return _ffn(h_tiles, row(ffn_norm_gain[layer]), w_ff1_16, w_ff2_16, row(final_norm_gain))
```

```python
import math

import jax
import jax.numpy as jnp
from jax import lax
from jax.experimental import pallas as pl
from jax.experimental.pallas import tpu as pltpu

D_MODEL = 2048
BATCH = 8
SEQ = 2048
CHUNK = 64
D_SSM = 1024
D_SGU = 1024
SSM_GROUP = 16
SSM_GROUPS = 64
SSM_STATE = 64
SGU_BLOCK = 128
SGU_HEADS = 8
SGU_HEAD_DIM = 128
D_FF = 4 * D_MODEL
D_IN = 2 * D_SSM + 2 * D_SGU
EPS = 1e-5

LANES = 128
MXU_WIDTH = 256

TIME_TILE = SGU_BLOCK
ROWS = BATCH * TIME_TILE
N_TILES = SEQ // TIME_TILE
GROUPS_PER_CHUNK = 8
N_CHUNKS = SSM_GROUPS // GROUPS_PER_CHUNK
CHUNK_IN = GROUPS_PER_CHUNK * SSM_GROUP
CHUNK_ST = GROUPS_PER_CHUNK * SSM_STATE
GATING_PLAN = ((0, 1), (2, 3), (4,), (5,), (6,), (7,), (), ())
SCAN_GROUP = 4
MXU_PIECES_PER_GROUP = 1
SIDE_STEPS_PER_GROUP = 1
FF_TILE = 1024
FF_PIECE = 512
PROJ_TILE = 1024
N_PROJ_TILES = D_IN // PROJ_TILE
X_PART = D_MODEL // N_PROJ_TILES
N_CAST_STEPS = N_TILES * N_PROJ_TILES
VMEM_LIMIT = 56 * 1024 * 1024
FFN_VMEM_LIMIT = 60 * 1024 * 1024

F32 = jnp.float32
BF16 = jnp.bfloat16


def _rms(x, gain):
    return x * lax.rsqrt(jnp.mean(x * x, axis=-1, keepdims=True) + EPS) * gain


def _gelu(x):
    c = math.sqrt(2.0 / math.pi)
    half_x = 0.5 * x
    return half_x + half_x * jnp.tanh(x * (c + (0.044715 * c) * (x * x)))


def _sigmoid(x):
    return 1.0 / (1.0 + jnp.exp(-x))


def _lanes(j, base=0):
    return slice(base + j * LANES, base + (j + 1) * LANES)


def _inproj_kernel(x0_ref, x1_ref, x2_ref, x3_ref, gain_ref, w_hbm, wff1_ref, wff2_ref, wout_ref,
                   o_ref, wff1_o, wff2_o, wout_o, hn_ref, w16_ref, wst_ref, w_sem):
    i = pl.program_id(0)
    j = pl.program_id(1)
    w_cols = pl.ds(pl.multiple_of(j * PROJ_TILE, PROJ_TILE), PROJ_TILE)

    def w_copy(tile):
        cols = pl.ds(pl.multiple_of(tile * PROJ_TILE, PROJ_TILE), PROJ_TILE)
        return pltpu.make_async_copy(w_hbm.at[:, cols], wst_ref, w_sem)

    @pl.when((i == 0) & (j == 0))
    def _():
        w_copy(0).start()

    @pl.when(i == 0)
    def _():
        w_copy(j).wait()
        w16_ref[:, w_cols] = wst_ref[...].astype(BF16)

    @pl.when((i == 0) & (j + 1 < N_PROJ_TILES))
    def _():
        w_copy(j + 1).start()

    @pl.when(j == 0)
    def _():
        parts = [r[...].reshape(ROWS, X_PART) for r in (x0_ref, x1_ref, x2_ref, x3_ref)]
        ssq = sum(jnp.sum(p * p, axis=-1, keepdims=True) for p in parts)
        scale = lax.rsqrt(ssq * (1.0 / D_MODEL) + EPS)
        for q, p in enumerate(parts):
            cols = slice(q * X_PART, (q + 1) * X_PART)
            hn_ref[:, cols] = (p * scale * gain_ref[:, cols]).astype(BF16)

    o_ref[...] = jnp.dot(hn_ref[...], w16_ref[:, w_cols], preferred_element_type=F32).astype(BF16)
    wff1_o[...] = wff1_ref[...].astype(BF16)
    wff2_o[...] = wff2_ref[...].astype(BF16)
    wout_o[...] = wout_ref[...].astype(BF16)


def _inproj(x, gain, w_in_tiles, w_ff1, w_ff2, w_out):
    def x_part(q):
        return pl.BlockSpec(
            (BATCH, TIME_TILE, X_PART),
            lambda i, j: (0, jnp.minimum(i + jnp.where(j > q, 1, 0), N_TILES - 1), q))

    def cast_rows(rows, width):
        return pl.BlockSpec((rows // N_CAST_STEPS, width),
                            lambda i, j: (i * N_PROJ_TILES + j, 0))

    return pl.pallas_call(
        _inproj_kernel,
        grid=(N_TILES, N_PROJ_TILES),
        in_specs=[
            x_part(0), x_part(1), x_part(2), x_part(3),
            pl.BlockSpec((1, D_MODEL), lambda i, j: (0, 0)),
            pl.BlockSpec(memory_space=pl.ANY),
            cast_rows(D_MODEL, D_FF), cast_rows(D_FF, D_MODEL), cast_rows(D_MODEL, D_MODEL),
        ],
        out_specs=[
            pl.BlockSpec((None, ROWS, PROJ_TILE), lambda i, j: (i, 0, j)),
            cast_rows(D_MODEL, D_FF), cast_rows(D_FF, D_MODEL), cast_rows(D_MODEL, D_MODEL),
        ],
        out_shape=[
            jax.ShapeDtypeStruct((N_TILES, ROWS, D_IN), BF16),
            jax.ShapeDtypeStruct((D_MODEL, D_FF), BF16),
            jax.ShapeDtypeStruct((D_FF, D_MODEL), BF16),
            jax.ShapeDtypeStruct((D_MODEL, D_MODEL), BF16),
        ],
        scratch_shapes=[pltpu.VMEM((ROWS, D_MODEL), BF16),
                        pltpu.VMEM((D_MODEL, D_IN), BF16),
                        pltpu.VMEM((D_MODEL, PROJ_TILE), F32),
                        pltpu.SemaphoreType.DMA(())],
        compiler_params=pltpu.CompilerParams(
            dimension_semantics=("arbitrary", "arbitrary"),
            vmem_limit_bytes=FFN_VMEM_LIMIT),
        name="inproj",
    )(x, x, x, x, gain, w_in_tiles, w_ff1, w_ff2, w_out)


def _sgu_steps(b, ub_ref, vb_ref, vgain_ref, w_heads, bs_ref, gain_b_ref, o_ref):
    rows = pl.ds(b * TIME_TILE, TIME_TILE)
    v = _rms(_gelu(vb_ref[rows, :].astype(F32)), vgain_ref[...]).astype(BF16)
    yield
    pieces = []
    for h in range(SGU_HEADS):
        cols = slice(h * SGU_HEAD_DIM, (h + 1) * SGU_HEAD_DIM)
        mixed = jnp.dot(w_heads[h], v[:, cols], preferred_element_type=F32) + bs_ref[h]
        pieces.append(_gelu(ub_ref[rows, cols].astype(F32)) * mixed)
        yield
    yb = jnp.concatenate(pieces, axis=-1)
    o_ref[rows, D_SSM:D_SSM + D_SGU] = _rms(yb, gain_b_ref[...]).astype(BF16)
    yield


def _drive(stream, n):
    for _ in range(n):
        try:
            next(stream)
        except StopIteration:
            return False
    return True


def _mixer_kernel(ua_ref, ga_ref, ub_ref, vb_ref, bd_ref, cd_ref, are_ref, aim_ref, dskip_ref,
                  vgain_ref, w_ref, bs_ref, gain_a_ref, gain_b_ref, o_ref,
                  state_ref, bu_ref, utb_ref, y_ref):
    @pl.when(pl.program_id(0) == 0)
    def _():
        state_ref[...] = jnp.zeros_like(state_ref)

    row_chunk = lax.broadcasted_iota(jnp.int32, (SGU_BLOCK, SGU_BLOCK), 0) // CHUNK
    col_chunk = lax.broadcasted_iota(jnp.int32, (SGU_BLOCK, SGU_BLOCK), 1) // CHUNK
    causal = col_chunk <= row_chunk
    w_heads = [jnp.where(causal, w_ref[h], 0.0).astype(BF16) for h in range(SGU_HEADS)]

    for b in range(BATCH):
        u_b = ua_ref[pl.ds(b * TIME_TILE, TIME_TILE), :].astype(F32)
        for s in range(N_CHUNKS):
            utb_ref[s, pl.ds(b, TIME_TILE, stride=BATCH), :] = u_b[:, _lanes(s)]

    half = CHUNK_ST // LANES
    readout_done = [False] * N_CHUNKS

    def input_matmul(kb):
        u16 = utb_ref[kb].astype(BF16)
        for jp in range(2 * CHUNK_ST // MXU_WIDTH):
            cols = slice(jp * MXU_WIDTH, (jp + 1) * MXU_WIDTH)
            bu_ref[kb % 2, :, cols] = jnp.dot(u16, bd_ref[kb, :, cols],
                                             preferred_element_type=F32)
            yield

    def readout(kb):
        halves = []
        for p in range(2):
            ks = slice(p * CHUNK_ST, (p + 1) * CHUNK_ST)
            halves.append(jnp.dot(bu_ref[kb % 2, :, ks].astype(BF16), cd_ref[kb, ks, :],
                                  preferred_element_type=F32))
            yield
        utb_ref[kb] = (dskip_ref[:, _lanes(kb)] * utb_ref[kb] + halves[0]) - halves[1]
        readout_done[kb] = True
        yield

    def gate_steps(kb):
        for b in range(BATCH):
            rows = pl.ds(b * TIME_TILE, TIME_TILE)
            y = utb_ref[kb, pl.ds(b, TIME_TILE, stride=BATCH), :]
            y_ref[rows, _lanes(kb)] = _gelu(y) * _sigmoid(ga_ref[rows, _lanes(kb)].astype(F32))
            yield

    def scan(kb):
        a_re = [are_ref[kb, :, _lanes(j)] for j in range(half)]
        a_im = [aim_ref[kb, :, _lanes(j)] for j in range(half)]
        s_re = [state_ref[kb, :, _lanes(j)] for j in range(half)]
        s_im = [state_ref[kb, :, _lanes(j, CHUNK_ST)] for j in range(half)]
        buf = kb % 2
        for t in range(TIME_TILE):
            rows = slice(t * BATCH, (t + 1) * BATCH)
            b_re = [bu_ref[buf, rows, _lanes(j)] for j in range(half)]
            b_im = [bu_ref[buf, rows, _lanes(j, CHUNK_ST)] for j in range(half)]
            n_re = [a_re[j] * s_re[j] - a_im[j] * s_im[j] + b_re[j] for j in range(half)]
            n_im = [a_re[j] * s_im[j] + a_im[j] * s_re[j] + b_im[j] for j in range(half)]
            for j in range(half):
                bu_ref[buf, rows, _lanes(j)] = n_re[j]
                bu_ref[buf, rows, _lanes(j, CHUNK_ST)] = n_im[j]
            s_re, s_im = n_re, n_im
            if (t + 1) % SCAN_GROUP == 0:
                yield
        for j in range(half):
            state_ref[kb, :, _lanes(j)] = s_re[j]
            state_ref[kb, :, _lanes(j, CHUNK_ST)] = s_im[j]

    def chain(*gens):
        for g in gens:
            yield from g

    for _ in input_matmul(0):
        pass
    for kb in range(N_CHUNKS):
        mxu = chain(*([readout(kb - 1)] if kb > 0 else []),
                    *([input_matmul(kb + 1)] if kb + 1 < N_CHUNKS else []))
        batches = GATING_PLAN[kb]
        side = chain(*[_sgu_steps(b, ub_ref, vb_ref, vgain_ref, w_heads, bs_ref, gain_b_ref, o_ref)
                       for b in batches],
                     *([gate_steps(kb - 1)] if kb > 0 else []))
        sgu_left = len(batches) * (SGU_HEADS + 2)
        for _ in scan(kb):
            _drive(mxu, MXU_PIECES_PER_GROUP)
            for _ in range(SIDE_STEPS_PER_GROUP):
                if sgu_left == 0 and kb > 0 and not readout_done[kb - 1]:
                    break
                _drive(side, 1)
                sgu_left = max(sgu_left - 1, 0)
        _drive(mxu, 1000)
        _drive(side, 1000)
    last = N_CHUNKS - 1
    _drive(readout(last), 1000)
    _drive(gate_steps(last), 1000)

    for b in range(BATCH):
        rows = pl.ds(b * TIME_TILE, TIME_TILE)
        o_ref[rows, 0:D_SSM] = _rms(y_ref[rows, :], gain_a_ref[...]).astype(BF16)


def _mixer(proj, bd, cd, a_re, a_im, d_skip, v_gain, w_s, bs, gain_a, gain_b):
    def const(shape):
        nd = len(shape)
        return pl.BlockSpec(shape, lambda i: (0,) * nd, pipeline_mode=pl.Buffered(1))

    def col(j):
        return pl.BlockSpec((None, ROWS, D_SSM), lambda i: (i, 0, j))

    return pl.pallas_call(
        _mixer_kernel,
        grid=(N_TILES,),
        in_specs=[
            col(0), col(1), col(2), col(3),
            const((N_CHUNKS, CHUNK_IN, 2 * CHUNK_ST)),
            const((N_CHUNKS, 2 * CHUNK_ST, CHUNK_IN)),
            const((N_CHUNKS, BATCH, CHUNK_ST)), const((N_CHUNKS, BATCH, CHUNK_ST)),
            const((1, D_SSM)), const((1, D_SGU)),
            const((SGU_HEADS, SGU_BLOCK, SGU_BLOCK)),
            const((SGU_HEADS, SGU_BLOCK, SGU_HEAD_DIM)),
            const((1, D_SSM)), const((1, D_SGU)),
        ],
        out_specs=pl.BlockSpec((None, ROWS, D_MODEL), lambda i: (i, 0, 0)),
        out_shape=jax.ShapeDtypeStruct((N_TILES, ROWS, D_MODEL), BF16),
        scratch_shapes=[
            pltpu.VMEM((N_CHUNKS, BATCH, 2 * CHUNK_ST), F32),
            pltpu.VMEM((2, ROWS, 2 * CHUNK_ST), F32),
            pltpu.VMEM((D_SSM // LANES, ROWS, LANES), F32),
            pltpu.VMEM((ROWS, D_SSM), F32),
        ],
        compiler_params=pltpu.CompilerParams(
            dimension_semantics=("arbitrary",),
            vmem_limit_bytes=VMEM_LIMIT),
        name="mixer",
    )(proj, proj, proj, proj, bd, cd, a_re, a_im, d_skip, v_gain, w_s, bs, gain_a, gain_b)


def _outproj_kernel(m_ref, w_ref, x_ref, o_ref):
    for c in range(D_MODEL // PROJ_TILE):
        cols = slice(c * PROJ_TILE, (c + 1) * PROJ_TILE)
        acc = jnp.dot(m_ref[...], w_ref[:, cols], preferred_element_type=F32)
        o_ref[:, cols] = x_ref[:, :, cols].reshape(ROWS, PROJ_TILE) + acc


def _outproj(mixed, w_out, x):
    return pl.pallas_call(
        _outproj_kernel,
        grid=(N_TILES,),
        in_specs=[
            pl.BlockSpec((None, ROWS, D_MODEL), lambda i: (i, 0, 0)),
            pl.BlockSpec((D_MODEL, D_MODEL), lambda i: (0, 0), pipeline_mode=pl.Buffered(1)),
            pl.BlockSpec((BATCH, TIME_TILE, D_MODEL), lambda i: (0, i, 0)),
        ],
        out_specs=pl.BlockSpec((None, ROWS, D_MODEL), lambda i: (i, 0, 0)),
        out_shape=jax.ShapeDtypeStruct((N_TILES, ROWS, D_MODEL), F32),
        compiler_params=pltpu.CompilerParams(
            dimension_semantics=("arbitrary",),
            vmem_limit_bytes=VMEM_LIMIT),
        name="outproj",
    )(mixed, w_out, x)


def _ffn_kernel(h_hbm, gain_ref, w1_ref, w2_ref, fgain_ref, o_ref, hn_ref, h_buf, h_sem):
    i = pl.program_id(0)
    k = pl.program_id(1)

    def h_copy(tile):
        return pltpu.make_async_copy(h_hbm.at[tile], h_buf, h_sem)

    @pl.when((i == 0) & (k == 0))
    def _():
        h_copy(0).start()

    @pl.when(k == 0)
    def _():
        h_copy(i).wait()
        h = h_buf[...]
        hn_ref[...] = _rms(h, gain_ref[...]).astype(BF16)
        o_ref[...] = h.reshape(BATCH, TIME_TILE, D_MODEL)

    @pl.when((k == 1) & (i + 1 < N_TILES))
    def _():
        h_copy(i + 1).start()

    acts = []
    for p in range(FF_TILE // FF_PIECE):
        z = jnp.dot(hn_ref[...], w1_ref[:, p * FF_PIECE:(p + 1) * FF_PIECE],
                    preferred_element_type=F32)
        z = jnp.maximum(z, 0.0)
        acts.append((z * z).astype(BF16))
    a = jnp.concatenate(acts, axis=1)
    for c in range(D_MODEL // FF_PIECE):
        cols = slice(c * FF_PIECE, (c + 1) * FF_PIECE)
        upd = jnp.dot(a, w2_ref[:, cols], preferred_element_type=F32)
        o_ref[:, :, cols] += upd.reshape(BATCH, TIME_TILE, FF_PIECE)

    @pl.when(k == pl.num_programs(1) - 1)
    def _():
        acc = o_ref[...].reshape(ROWS, D_MODEL)
        o_ref[...] = _rms(acc, fgain_ref[...]).reshape(BATCH, TIME_TILE, D_MODEL)


def _ffn(h, gain, w1, w2, fgain):
    return pl.pallas_call(
        _ffn_kernel,
        grid=(N_TILES, D_FF // FF_TILE),
        in_specs=[
            pl.BlockSpec(memory_space=pl.ANY),
            pl.BlockSpec((1, D_MODEL), lambda i, k: (0, 0)),
            pl.BlockSpec((D_MODEL, FF_TILE), lambda i, k: (0, k)),
            pl.BlockSpec((FF_TILE, D_MODEL), lambda i, k: (k, 0)),
            pl.BlockSpec((1, D_MODEL), lambda i, k: (0, 0)),
        ],
        out_specs=pl.BlockSpec((BATCH, TIME_TILE, D_MODEL), lambda i, k: (0, i, 0)),
        out_shape=jax.ShapeDtypeStruct((BATCH, SEQ, D_MODEL), F32),
        scratch_shapes=[pltpu.VMEM((ROWS, D_MODEL), BF16),
                        pltpu.VMEM((ROWS, D_MODEL), F32),
                        pltpu.SemaphoreType.DMA(())],
        compiler_params=pltpu.CompilerParams(
            dimension_semantics=("arbitrary", "arbitrary"),
            vmem_limit_bytes=FFN_VMEM_LIMIT),
        name="ffn",
    )(h, gain, w1, w2, fgain)


def _ssm_params(a_re, a_im, log_dt, b_re, b_im, c_re, c_im):
    dt = jnp.exp(log_dt.astype(F32))[:, None]
    lam_re = jnp.minimum(a_re.astype(F32), -1e-4)
    lam_im = a_im.astype(F32)
    decay = jnp.exp(lam_re * dt)
    abar_re = decay * jnp.cos(lam_im * dt)
    abar_im = decay * jnp.sin(lam_im * dt)
    den = lam_re * lam_re + lam_im * lam_im
    num_re = abar_re - 1.0
    coef_re = (num_re * lam_re + abar_im * lam_im) / den
    coef_im = (abar_im * lam_re - num_re * lam_im) / den
    br = b_re.astype(F32)
    bi = b_im.astype(F32)
    bbar_re = coef_re[..., None] * br - coef_im[..., None] * bi
    bbar_im = coef_re[..., None] * bi + coef_im[..., None] * br

    g = GROUPS_PER_CHUNK

    def chunked(m):
        return jnp.swapaxes(m.reshape(N_CHUNKS, g, m.shape[1], m.shape[2]), 2, 3)

    b_parts = jnp.stack([chunked(bbar_re), chunked(bbar_im)], axis=3)
    b_rows = b_parts.reshape(N_CHUNKS, CHUNK_IN, 2, 1, SSM_STATE)
    b_tiled = jnp.broadcast_to(b_rows, (N_CHUNKS, CHUNK_IN, 2, g, SSM_STATE))
    row_group = lax.broadcasted_iota(jnp.int32, b_tiled.shape, 1) // SSM_GROUP
    col_group = lax.broadcasted_iota(jnp.int32, b_tiled.shape, 3)
    bd = jnp.where(row_group == col_group, b_tiled, 0.0)
    bd = bd.reshape(N_CHUNKS, CHUNK_IN, 2 * CHUNK_ST).astype(BF16)

    c_parts = jnp.stack([chunked(c_re.astype(F32)), chunked(c_im.astype(F32))], axis=1)
    c_rows = c_parts.reshape(N_CHUNKS, 2 * CHUNK_ST, 1, SSM_GROUP)
    c_tiled = jnp.broadcast_to(c_rows, (N_CHUNKS, 2 * CHUNK_ST, g, SSM_GROUP))
    row_group = (lax.broadcasted_iota(jnp.int32, c_tiled.shape, 1) % CHUNK_ST) // SSM_STATE
    col_group = lax.broadcasted_iota(jnp.int32, c_tiled.shape, 2)
    cd = jnp.where(row_group == col_group, c_tiled, 0.0)
    cd = cd.reshape(N_CHUNKS, 2 * CHUNK_ST, CHUNK_IN).astype(BF16)

    def per_chunk(a):
        a = a.reshape(N_CHUNKS, 1, CHUNK_ST)
        return jnp.broadcast_to(a, (N_CHUNKS, BATCH, CHUNK_ST))

    return bd, cd, per_chunk(abar_re), per_chunk(abar_im)


def kernel(x, w_in, ssm_a_re, ssm_a_im, ssm_log_dt, ssm_b_re, ssm_b_im, ssm_c_re, ssm_c_im,
           ssm_d, sgu_v_gain, sgu_w, sgu_b, out_gain_ssm, out_gain_sgu, w_out,
           mix_norm_gain, ffn_norm_gain, w_ff1, w_ff2, final_norm_gain):
    row = lambda v: v.astype(F32).reshape(1, -1)
    assert x.shape == (BATCH, SEQ, D_MODEL) and w_in.shape[0] == 1, "single-layer block only"
    layer = 0
    bd, cd, a_re, a_im = _ssm_params(
        ssm_a_re[layer], ssm_a_im[layer], ssm_log_dt[layer], ssm_b_re[layer],
        ssm_b_im[layer], ssm_c_re[layer], ssm_c_im[layer])
    bs = jnp.broadcast_to(sgu_b[layer].astype(F32)[:, :, None],
                          (SGU_HEADS, SGU_BLOCK, SGU_HEAD_DIM))
    proj, w_ff1_16, w_ff2_16, w_out_16 = _inproj(
        x, row(mix_norm_gain[layer]), w_in[layer], w_ff1[layer], w_ff2[layer],
        w_out[layer])
    mixed = _mixer(proj, bd, cd, a_re, a_im, row(ssm_d[layer]),
                   row(sgu_v_gain[layer]), sgu_w[layer].astype(F32), bs,
                   row(out_gain_ssm[layer]), row(out_gain_sgu[layer]))
    h_tiles = _outproj(mixed, w_out_16, x)
    return _ffn(h_tiles, row(ffn_norm_gain[layer]), w_ff1_16, w_ff2_16, row(final_norm_gain))
```

```python
import math

import jax
import jax.numpy as jnp
from jax import lax
from jax.experimental import pallas as pl
from jax.experimental.pallas import tpu as pltpu

D_MODEL = 2048
BATCH = 8
SEQ = 2048
CHUNK = 64
D_SSM = 1024
D_SGU = 1024
SSM_GROUP = 16
SSM_GROUPS = 64
SSM_STATE = 64
SGU_BLOCK = 128
SGU_HEADS = 8
SGU_HEAD_DIM = 128
D_FF = 4 * D_MODEL
D_IN = 2 * D_SSM + 2 * D_SGU
EPS = 1e-5

LANES = 128
MXU_WIDTH = 256

TIME_TILE = SGU_BLOCK
ROWS = BATCH * TIME_TILE
N_TILES = SEQ // TIME_TILE
GROUPS_PER_CHUNK = 8
N_CHUNKS = SSM_GROUPS // GROUPS_PER_CHUNK
CHUNK_IN = GROUPS_PER_CHUNK * SSM_GROUP
CHUNK_ST = GROUPS_PER_CHUNK * SSM_STATE
GATING_PLAN = ((0, 1), (2, 3), (4,), (5,), (6,), (7,), (), ())
SCAN_GROUP = 4
MXU_PIECES_PER_GROUP = 1
SIDE_STEPS_PER_GROUP = 1
FF_TILE = 1024
FF_PIECE = 512
PROJ_TILE = 1024
N_PROJ_TILES = D_IN // PROJ_TILE
X_PART = D_MODEL // N_PROJ_TILES
N_CAST_STEPS = N_TILES * N_PROJ_TILES
VMEM_LIMIT = 56 * 1024 * 1024
INPROJ_VMEM_LIMIT = 60 * 1024 * 1024

F32 = jnp.float32
BF16 = jnp.bfloat16


def _rms(x, gain):
    return x * lax.rsqrt(jnp.mean(x * x, axis=-1, keepdims=True) + EPS) * gain


def _gelu(x):
    c = math.sqrt(2.0 / math.pi)
    half_x = 0.5 * x
    return half_x + half_x * jnp.tanh(x * (c + (0.044715 * c) * (x * x)))


def _sigmoid(x):
    return 1.0 / (1.0 + jnp.exp(-x))


def _lanes(j, base=0):
    return slice(base + j * LANES, base + (j + 1) * LANES)


def _inproj_kernel(x0_ref, x1_ref, x2_ref, x3_ref, gain_ref, w_hbm, wff1_ref, wff2_ref, wout_ref,
                   o_ref, wff1_o, wff2_o, wout_o, hn_ref, w16_ref, wst_ref, w_sem):
    i = pl.program_id(0)
    j = pl.program_id(1)
    w_cols = pl.ds(pl.multiple_of(j * PROJ_TILE, PROJ_TILE), PROJ_TILE)

    def w_copy(tile):
        cols = pl.ds(pl.multiple_of(tile * PROJ_TILE, PROJ_TILE), PROJ_TILE)
        return pltpu.make_async_copy(w_hbm.at[:, cols], wst_ref, w_sem)

    @pl.when((i == 0) & (j == 0))
    def _():
        w_copy(0).start()

    @pl.when(i == 0)
    def _():
        w_copy(j).wait()
        w16_ref[:, w_cols] = wst_ref[...].astype(BF16)

    @pl.when((i == 0) & (j + 1 < N_PROJ_TILES))
    def _():
        w_copy(j + 1).start()

    @pl.when(j == 0)
    def _():
        parts = [r[...].reshape(ROWS, X_PART) for r in (x0_ref, x1_ref, x2_ref, x3_ref)]
        ssq = sum(jnp.sum(p * p, axis=-1, keepdims=True) for p in parts)
        scale = lax.rsqrt(ssq * (1.0 / D_MODEL) + EPS)
        for q, p in enumerate(parts):
            cols = slice(q * X_PART, (q + 1) * X_PART)
            hn_ref[:, cols] = (p * scale * gain_ref[:, cols]).astype(BF16)

    o_ref[...] = jnp.dot(hn_ref[...], w16_ref[:, w_cols], preferred_element_type=F32).astype(BF16)
    wff1_o[...] = wff1_ref[...].astype(BF16)
    wff2_o[...] = wff2_ref[...].astype(BF16)
    wout_o[...] = wout_ref[...].astype(BF16)


def _inproj(x, gain, w_in_tiles, w_ff1, w_ff2, w_out):
    def x_part(q):
        return pl.BlockSpec(
            (BATCH, TIME_TILE, X_PART),
            lambda i, j: (0, jnp.minimum(i + jnp.where(j > q, 1, 0), N_TILES - 1), q))

    def cast_rows(rows, width):
        return pl.BlockSpec((rows // N_CAST_STEPS, width),
                            lambda i, j: (i * N_PROJ_TILES + j, 0))

    return pl.pallas_call(
        _inproj_kernel,
        grid=(N_TILES, N_PROJ_TILES),
        in_specs=[
            x_part(0), x_part(1), x_part(2), x_part(3),
            pl.BlockSpec((1, D_MODEL), lambda i, j: (0, 0)),
            pl.BlockSpec(memory_space=pl.ANY),
            cast_rows(D_MODEL, D_FF), cast_rows(D_FF, D_MODEL), cast_rows(D_MODEL, D_MODEL),
        ],
        out_specs=[
            pl.BlockSpec((None, ROWS, PROJ_TILE), lambda i, j: (i, 0, j)),
            cast_rows(D_MODEL, D_FF), cast_rows(D_FF, D_MODEL), cast_rows(D_MODEL, D_MODEL),
        ],
        out_shape=[
            jax.ShapeDtypeStruct((N_TILES, ROWS, D_IN), BF16),
            jax.ShapeDtypeStruct((D_MODEL, D_FF), BF16),
            jax.ShapeDtypeStruct((D_FF, D_MODEL), BF16),
            jax.ShapeDtypeStruct((D_MODEL, D_MODEL), BF16),
        ],
        scratch_shapes=[pltpu.VMEM((ROWS, D_MODEL), BF16),
                        pltpu.VMEM((D_MODEL, D_IN), BF16),
                        pltpu.VMEM((D_MODEL, PROJ_TILE), F32),
                        pltpu.SemaphoreType.DMA(())],
        compiler_params=pltpu.CompilerParams(
            dimension_semantics=("arbitrary", "arbitrary"),
            vmem_limit_bytes=INPROJ_VMEM_LIMIT),
        name="inproj",
    )(x, x, x, x, gain, w_in_tiles, w_ff1, w_ff2, w_out)


def _sgu_steps(b, ub_ref, vb_ref, vgain_ref, w_heads, bs_ref, gain_b_ref, o_ref):
    rows = pl.ds(b * TIME_TILE, TIME_TILE)
    v = _rms(_gelu(vb_ref[rows, :].astype(F32)), vgain_ref[...]).astype(BF16)
    yield
    pieces = []
    for h in range(SGU_HEADS):
        cols = slice(h * SGU_HEAD_DIM, (h + 1) * SGU_HEAD_DIM)
        mixed = jnp.dot(w_heads[h], v[:, cols], preferred_element_type=F32) + bs_ref[h]
        pieces.append(_gelu(ub_ref[rows, cols].astype(F32)) * mixed)
        yield
    yb = jnp.concatenate(pieces, axis=-1)
    o_ref[rows, D_SSM:D_SSM + D_SGU] = _rms(yb, gain_b_ref[...]).astype(BF16)
    yield


def _drive(stream, n):
    for _ in range(n):
        try:
            next(stream)
        except StopIteration:
            return False
    return True


def _mixer_kernel(ua_ref, ga_ref, ub_ref, vb_ref, bd_ref, cd_ref, are_ref, aim_ref, dskip_ref,
                  vgain_ref, w_ref, bs_ref, gain_a_ref, gain_b_ref, o_ref,
                  state_ref, bu_ref, utb_ref, y_ref):
    @pl.when(pl.program_id(0) == 0)
    def _():
        state_ref[...] = jnp.zeros_like(state_ref)

    row_chunk = lax.broadcasted_iota(jnp.int32, (SGU_BLOCK, SGU_BLOCK), 0) // CHUNK
    col_chunk = lax.broadcasted_iota(jnp.int32, (SGU_BLOCK, SGU_BLOCK), 1) // CHUNK
    causal = col_chunk <= row_chunk
    w_heads = [jnp.where(causal, w_ref[h], 0.0).astype(BF16) for h in range(SGU_HEADS)]

    for b in range(BATCH):
        u_b = ua_ref[pl.ds(b * TIME_TILE, TIME_TILE), :].astype(F32)
        for s in range(N_CHUNKS):
            utb_ref[s, pl.ds(b, TIME_TILE, stride=BATCH), :] = u_b[:, _lanes(s)]

    half = CHUNK_ST // LANES
    readout_done = [False] * N_CHUNKS

    def input_matmul(kb):
        u16 = utb_ref[kb].astype(BF16)
        for jp in range(2 * CHUNK_ST // MXU_WIDTH):
            cols = slice(jp * MXU_WIDTH, (jp + 1) * MXU_WIDTH)
            bu_ref[kb % 2, :, cols] = jnp.dot(u16, bd_ref[kb, :, cols],
                                             preferred_element_type=F32)
            yield

    def readout(kb):
        halves = []
        for p in range(2):
            ks = slice(p * CHUNK_ST, (p + 1) * CHUNK_ST)
            halves.append(jnp.dot(bu_ref[kb % 2, :, ks].astype(BF16), cd_ref[kb, ks, :],
                                  preferred_element_type=F32))
            yield
        utb_ref[kb] = (dskip_ref[:, _lanes(kb)] * utb_ref[kb] + halves[0]) - halves[1]
        readout_done[kb] = True
        yield

    def gate_steps(kb):
        for b in range(BATCH):
            rows = pl.ds(b * TIME_TILE, TIME_TILE)
            y = utb_ref[kb, pl.ds(b, TIME_TILE, stride=BATCH), :]
            y_ref[rows, _lanes(kb)] = _gelu(y) * _sigmoid(ga_ref[rows, _lanes(kb)].astype(F32))
            yield

    def scan(kb):
        a_re = [are_ref[kb, :, _lanes(j)] for j in range(half)]
        a_im = [aim_ref[kb, :, _lanes(j)] for j in range(half)]
        s_re = [state_ref[kb, :, _lanes(j)] for j in range(half)]
        s_im = [state_ref[kb, :, _lanes(j, CHUNK_ST)] for j in range(half)]
        buf = kb % 2
        for t in range(TIME_TILE):
            rows = slice(t * BATCH, (t + 1) * BATCH)
            b_re = [bu_ref[buf, rows, _lanes(j)] for j in range(half)]
            b_im = [bu_ref[buf, rows, _lanes(j, CHUNK_ST)] for j in range(half)]
            n_re = [a_re[j] * s_re[j] - a_im[j] * s_im[j] + b_re[j] for j in range(half)]
            n_im = [a_re[j] * s_im[j] + a_im[j] * s_re[j] + b_im[j] for j in range(half)]
            for j in range(half):
                bu_ref[buf, rows, _lanes(j)] = n_re[j]
                bu_ref[buf, rows, _lanes(j, CHUNK_ST)] = n_im[j]
            s_re, s_im = n_re, n_im
            if (t + 1) % SCAN_GROUP == 0:
                yield
        for j in range(half):
            state_ref[kb, :, _lanes(j)] = s_re[j]
            state_ref[kb, :, _lanes(j, CHUNK_ST)] = s_im[j]

    def chain(*gens):
        for g in gens:
            yield from g

    for _ in input_matmul(0):
        pass
    for kb in range(N_CHUNKS):
        mxu = chain(*([readout(kb - 1)] if kb > 0 else []),
                    *([input_matmul(kb + 1)] if kb + 1 < N_CHUNKS else []))
        batches = GATING_PLAN[kb]
        side = chain(*[_sgu_steps(b, ub_ref, vb_ref, vgain_ref, w_heads, bs_ref, gain_b_ref, o_ref)
                       for b in batches],
                     *([gate_steps(kb - 1)] if kb > 0 else []))
        sgu_left = len(batches) * (SGU_HEADS + 2)
        for _ in scan(kb):
            _drive(mxu, MXU_PIECES_PER_GROUP)
            for _ in range(SIDE_STEPS_PER_GROUP):
                if sgu_left == 0 and kb > 0 and not readout_done[kb - 1]:
                    break
                _drive(side, 1)
                sgu_left = max(sgu_left - 1, 0)
        _drive(mxu, 1000)
        _drive(side, 1000)
    last = N_CHUNKS - 1
    _drive(readout(last), 1000)
    _drive(gate_steps(last), 1000)

    for b in range(BATCH):
        rows = pl.ds(b * TIME_TILE, TIME_TILE)
        o_ref[rows, 0:D_SSM] = _rms(y_ref[rows, :], gain_a_ref[...]).astype(BF16)


def _mixer(proj, bd, cd, a_re, a_im, d_skip, v_gain, w_s, bs, gain_a, gain_b):
    def const(shape):
        nd = len(shape)
        return pl.BlockSpec(shape, lambda i: (0,) * nd, pipeline_mode=pl.Buffered(1))

    def col(j):
        return pl.BlockSpec((None, ROWS, D_SSM), lambda i: (i, 0, j))

    return pl.pallas_call(
        _mixer_kernel,
        grid=(N_TILES,),
        in_specs=[
            col(0), col(1), col(2), col(3),
            const((N_CHUNKS, CHUNK_IN, 2 * CHUNK_ST)),
            const((N_CHUNKS, 2 * CHUNK_ST, CHUNK_IN)),
            const((N_CHUNKS, BATCH, CHUNK_ST)), const((N_CHUNKS, BATCH, CHUNK_ST)),
            const((1, D_SSM)), const((1, D_SGU)),
            const((SGU_HEADS, SGU_BLOCK, SGU_BLOCK)),
            const((SGU_HEADS, SGU_BLOCK, SGU_HEAD_DIM)),
            const((1, D_SSM)), const((1, D_SGU)),
        ],
        out_specs=pl.BlockSpec((None, ROWS, D_MODEL), lambda i: (i, 0, 0)),
        out_shape=jax.ShapeDtypeStruct((N_TILES, ROWS, D_MODEL), BF16),
        scratch_shapes=[
            pltpu.VMEM((N_CHUNKS, BATCH, 2 * CHUNK_ST), F32),
            pltpu.VMEM((2, ROWS, 2 * CHUNK_ST), F32),
            pltpu.VMEM((D_SSM // LANES, ROWS, LANES), F32),
            pltpu.VMEM((ROWS, D_SSM), F32),
        ],
        compiler_params=pltpu.CompilerParams(
            dimension_semantics=("arbitrary",),
            vmem_limit_bytes=VMEM_LIMIT),
        name="mixer",
    )(proj, proj, proj, proj, bd, cd, a_re, a_im, d_skip, v_gain, w_s, bs, gain_a, gain_b)


def _outproj_kernel(m_ref, w_ref, x_ref, o_ref):
    for c in range(D_MODEL // PROJ_TILE):
        cols = slice(c * PROJ_TILE, (c + 1) * PROJ_TILE)
        acc = jnp.dot(m_ref[...], w_ref[:, cols], preferred_element_type=F32)
        o_ref[:, cols] = x_ref[:, :, cols].reshape(ROWS, PROJ_TILE) + acc


def _outproj(mixed, w_out, x):
    return pl.pallas_call(
        _outproj_kernel,
        grid=(N_TILES,),
        in_specs=[
            pl.BlockSpec((None, ROWS, D_MODEL), lambda i: (i, 0, 0)),
            pl.BlockSpec((D_MODEL, D_MODEL), lambda i: (0, 0), pipeline_mode=pl.Buffered(1)),
            pl.BlockSpec((BATCH, TIME_TILE, D_MODEL), lambda i: (0, i, 0)),
        ],
        out_specs=pl.BlockSpec((None, ROWS, D_MODEL), lambda i: (i, 0, 0)),
        out_shape=jax.ShapeDtypeStruct((N_TILES, ROWS, D_MODEL), F32),
        compiler_params=pltpu.CompilerParams(
            dimension_semantics=("arbitrary",),
            vmem_limit_bytes=VMEM_LIMIT),
        name="outproj",
    )(mixed, w_out, x)


def _ffn_kernel(h_hbm, gain_ref, w1_ref, w2_ref, fgain_ref, o_ref, hn_ref, h_buf, h_sem):
    i = pl.program_id(0)
    k = pl.program_id(1)

    def h_copy(tile):
        return pltpu.make_async_copy(h_hbm.at[tile], h_buf, h_sem)

    @pl.when((i == 0) & (k == 0))
    def _():
        h_copy(0).start()

    @pl.when(k == 0)
    def _():
        h_copy(i).wait()
        h = h_buf[...]
        hn_ref[...] = _rms(h, gain_ref[...]).astype(BF16)
        o_ref[...] = h.reshape(BATCH, TIME_TILE, D_MODEL)

    @pl.when((k == 1) & (i + 1 < N_TILES))
    def _():
        h_copy(i + 1).start()

    acts = []
    for p in range(FF_TILE // FF_PIECE):
        z = jnp.dot(hn_ref[...], w1_ref[:, p * FF_PIECE:(p + 1) * FF_PIECE],
                    preferred_element_type=F32)
        z = jnp.maximum(z, 0.0)
        acts.append((z * z).astype(BF16))
    a = jnp.concatenate(acts, axis=1)
    for c in range(D_MODEL // FF_PIECE):
        cols = slice(c * FF_PIECE, (c + 1) * FF_PIECE)
        upd = jnp.dot(a, w2_ref[:, cols], preferred_element_type=F32)
        o_ref[:, :, cols] += upd.reshape(BATCH, TIME_TILE, FF_PIECE)

    @pl.when(k == pl.num_programs(1) - 1)
    def _():
        acc = o_ref[...].reshape(ROWS, D_MODEL)
        o_ref[...] = _rms(acc, fgain_ref[...]).reshape(BATCH, TIME_TILE, D_MODEL)


def _ffn(h, gain, w1, w2, fgain):
    return pl.pallas_call(
        _ffn_kernel,
        grid=(N_TILES, D_FF // FF_TILE),
        in_specs=[
            pl.BlockSpec(memory_space=pl.ANY),
            pl.BlockSpec((1, D_MODEL), lambda i, k: (0, 0)),
            pl.BlockSpec((D_MODEL, FF_TILE), lambda i, k: (0, k)),
            pl.BlockSpec((FF_TILE, D_MODEL), lambda i, k: (k, 0)),
            pl.BlockSpec((1, D_MODEL), lambda i, k: (0, 0)),
        ],
        out_specs=pl.BlockSpec((BATCH, TIME_TILE, D_MODEL), lambda i, k: (0, i, 0)),
        out_shape=jax.ShapeDtypeStruct((BATCH, SEQ, D_MODEL), F32),
        scratch_shapes=[pltpu.VMEM((ROWS, D_MODEL), BF16),
                        pltpu.VMEM((ROWS, D_MODEL), F32),
                        pltpu.SemaphoreType.DMA(())],
        compiler_params=pltpu.CompilerParams(
            dimension_semantics=("arbitrary", "arbitrary"),
            vmem_limit_bytes=VMEM_LIMIT),
        name="ffn",
    )(h, gain, w1, w2, fgain)


def _ssm_params(a_re, a_im, log_dt, b_re, b_im, c_re, c_im):
    dt = jnp.exp(log_dt.astype(F32))[:, None]
    lam_re = jnp.minimum(a_re.astype(F32), -1e-4)
    lam_im = a_im.astype(F32)
    decay = jnp.exp(lam_re * dt)
    abar_re = decay * jnp.cos(lam_im * dt)
    abar_im = decay * jnp.sin(lam_im * dt)
    den = lam_re * lam_re + lam_im * lam_im
    num_re = abar_re - 1.0
    coef_re = (num_re * lam_re + abar_im * lam_im) / den
    coef_im = (abar_im * lam_re - num_re * lam_im) / den
    br = b_re.astype(F32)
    bi = b_im.astype(F32)
    bbar_re = coef_re[..., None] * br - coef_im[..., None] * bi
    bbar_im = coef_re[..., None] * bi + coef_im[..., None] * br

    g = GROUPS_PER_CHUNK

    def chunked(m):
        return jnp.swapaxes(m.reshape(N_CHUNKS, g, m.shape[1], m.shape[2]), 2, 3)

    b_parts = jnp.stack([chunked(bbar_re), chunked(bbar_im)], axis=3)
    b_rows = b_parts.reshape(N_CHUNKS, CHUNK_IN, 2, 1, SSM_STATE)
    b_tiled = jnp.broadcast_to(b_rows, (N_CHUNKS, CHUNK_IN, 2, g, SSM_STATE))
    row_group = lax.broadcasted_iota(jnp.int32, b_tiled.shape, 1) // SSM_GROUP
    col_group = lax.broadcasted_iota(jnp.int32, b_tiled.shape, 3)
    bd = jnp.where(row_group == col_group, b_tiled, 0.0)
    bd = bd.reshape(N_CHUNKS, CHUNK_IN, 2 * CHUNK_ST).astype(BF16)

    c_parts = jnp.stack([chunked(c_re.astype(F32)), chunked(c_im.astype(F32))], axis=1)
    c_rows = c_parts.reshape(N_CHUNKS, 2 * CHUNK_ST, 1, SSM_GROUP)
    c_tiled = jnp.broadcast_to(c_rows, (N_CHUNKS, 2 * CHUNK_ST, g, SSM_GROUP))
    row_group = (lax.broadcasted_iota(jnp.int32, c_tiled.shape, 1) % CHUNK_ST) // SSM_STATE
    col_group = lax.broadcasted_iota(jnp.int32, c_tiled.shape, 2)
    cd = jnp.where(row_group == col_group, c_tiled, 0.0)
    cd = cd.reshape(N_CHUNKS, 2 * CHUNK_ST, CHUNK_IN).astype(BF16)

    def per_chunk(a):
        a = a.reshape(N_CHUNKS, 1, CHUNK_ST)
        return jnp.broadcast_to(a, (N_CHUNKS, BATCH, CHUNK_ST))

    return bd, cd, per_chunk(abar_re), per_chunk(abar_im)


def kernel(x, w_in, ssm_a_re, ssm_a_im, ssm_log_dt, ssm_b_re, ssm_b_im, ssm_c_re, ssm_c_im,
           ssm_d, sgu_v_gain, sgu_w, sgu_b, out_gain_ssm, out_gain_sgu, w_out,
           mix_norm_gain, ffn_norm_gain, w_ff1, w_ff2, final_norm_gain):
    row = lambda v: v.astype(F32).reshape(1, -1)
    assert x.shape == (BATCH, SEQ, D_MODEL) and w_in.shape[0] == 1, "single-layer block only"
    layer = 0
    bd, cd, a_re, a_im = _ssm_params(
        ssm_a_re[layer], ssm_a_im[layer], ssm_log_dt[layer], ssm_b_re[layer],
        ssm_b_im[layer], ssm_c_re[layer], ssm_c_im[layer])
    bs = jnp.broadcast_to(sgu_b[layer].astype(F32)[:, :, None],
                          (SGU_HEADS, SGU_BLOCK, SGU_HEAD_DIM))
    proj, w_ff1_16, w_ff2_16, w_out_16 = _inproj(
        x, row(mix_norm_gain[layer]), w_in[layer], w_ff1[layer], w_ff2[layer],
        w_out[layer])
    mixed = _mixer(proj, bd, cd, a_re, a_im, row(ssm_d[layer]),
                   row(sgu_v_gain[layer]), sgu_w[layer].astype(F32), bs,
                   row(out_gain_ssm[layer]), row(out_gain_sgu[layer]))
    h_tiles = _outproj(mixed, w_out_16, x)
    return _ffn(h_tiles, row(ffn_norm_gain[layer]), w_ff1_16, w_ff2_16, row(final_norm_gain))
```
